```python
import jax, jax.numpy as jnp
from jax import lax
import numpy as np

D_MODEL = 1024
BATCH = 8
SEQ = 4096
DEPTH = 4

A_WIDTH = D_MODEL
A_GROUPS = 8
A_GROUP_DIM = A_WIDTH // A_GROUPS
A_CHUNK = 128
B_HEADS = 8
B_DK = 128
B_DV = 128
B_CONV = 4
B_CHUNK = 64
FFN_HIDDEN = -(-8 * D_MODEL // (3 * 256)) * 256
IN_UV = 2 * A_WIDTH
QK_WIDTH = B_HEADS * B_DK
V_WIDTH = B_HEADS * B_DV
IN_QKV = 2 * QK_WIDTH + V_WIDTH
IN_GATES = 2 * D_MODEL
IN_TOTAL = IN_UV + IN_QKV + V_WIDTH + 2 * B_HEADS + IN_GATES
NORM_EPS = 1e-6

kernel_name = "hybrid_gmlp_gdn_adaln_block"


def rms_norm(x, w):
    xf = x.astype(jnp.float32)
    y = xf * lax.rsqrt(jnp.mean(xf * xf, axis=-1, keepdims=True) + NORM_EPS)
    return (y * w.astype(jnp.float32)).astype(x.dtype)


def l2_norm(x):
    return x * lax.rsqrt(jnp.sum(x * x, axis=-1, keepdims=True) + NORM_EPS)


def causal_dwconv(x, w):
    K, C = w.shape
    return lax.conv_general_dilated(
        x, w[:, None, :], window_strides=(1,), padding=[(K - 1, 0)],
        dimension_numbers=("NWC", "WIO", "NWC"), feature_group_count=C)


def chunked_spatial_gating(uv, w_s, b_s, g_v):
    Bn, T, _ = uv.shape
    u, v = jnp.split(uv, 2, axis=-1)
    v = v.reshape(Bn, T // A_CHUNK, A_CHUNK, A_GROUPS, A_GROUP_DIM)
    v = rms_norm(v, g_v.reshape(A_GROUPS, A_GROUP_DIM))
    causal = jnp.tril(jnp.ones((A_CHUNK, A_CHUNK), dtype=bool))
    w = jnp.where(causal, w_s, 0)
    s = jnp.einsum("gts,bnsgd->bntgd", w, v) + jnp.swapaxes(b_s, 0, 1)[None, None, :, :, None]
    return u * s.reshape(Bn, T, A_WIDTH)


def gated_delta_rule(q, k, v, g, beta):
    Bn, T, H, DK = q.shape
    DV = v.shape[-1]
    C = B_CHUNK
    N = T // C
    q = q * DK ** -0.5

    def chunk4(t):
        return t.reshape(Bn, N, C, H, t.shape[-1]).transpose(0, 3, 1, 2, 4)

    def chunk3(t):
        return t.reshape(Bn, N, C, H).transpose(0, 3, 1, 2)

    qc, kc, vc = chunk4(q), chunk4(k), chunk4(v)
    gc, bc = chunk3(g), chunk3(beta)
    gam = jnp.cumsum(gc, axis=-1)
    causal = jnp.tril(jnp.ones((C, C), dtype=bool))
    strict = jnp.tril(jnp.ones((C, C), dtype=bool), k=-1)
    decay_mat = jnp.exp(jnp.where(causal, gam[..., :, None] - gam[..., None, :], -jnp.inf))
    k_beta = kc * bc[..., None]
    kk = jnp.einsum("bhnid,bhnjd->bhnij", k_beta, kc)
    a_mat = jnp.where(strict, kk * decay_mat, 0.0)
    eye = jnp.broadcast_to(jnp.eye(C, dtype=a_mat.dtype), a_mat.shape)
    t_mat = lax.linalg.triangular_solve(a_mat, eye, left_side=True, lower=True, unit_diagonal=True)
    u_val = t_mat @ (vc * bc[..., None])
    w_key = t_mat @ (k_beta * jnp.exp(gam)[..., None])
    qk = jnp.einsum("bhnid,bhnjd->bhnij", qc, kc) * decay_mat
    q_dec = qc * jnp.exp(gam)[..., None]
    k_dec = kc * jnp.exp(gam[..., -1:] - gam)[..., None]
    g_last = jnp.exp(gam[..., -1])

    def step(S, xs):
        qd, kd, uv_, wk, a, gl = xs
        v_new = uv_ - wk @ S
        o = qd @ S + a @ v_new
        S = S * gl[..., None, None] + jnp.swapaxes(kd, -1, -2) @ v_new
        return S, o

    xs = tuple(jnp.moveaxis(t, 2, 0) for t in (q_dec, k_dec, u_val, w_key, qk, g_last))
    S0 = jnp.zeros((Bn, H, DK, DV), dtype=q.dtype)
    _, o = lax.scan(step, S0, xs)
    return o.transpose(1, 0, 3, 2, 4).reshape(Bn, T, H, DV)


def gated_deltanet(qkv, z, b_raw, a_raw, conv_w, a_log, dt_bias, g_o):
    Bn, T, _ = qkv.shape
    dt = qkv.dtype
    qkv = jax.nn.silu(causal_dwconv(qkv, conv_w)).astype(jnp.float32)
    q, k, v = jnp.split(qkv, [QK_WIDTH, 2 * QK_WIDTH], axis=-1)
    q = l2_norm(q.reshape(Bn, T, B_HEADS, B_DK))
    k = l2_norm(k.reshape(Bn, T, B_HEADS, B_DK))
    v = v.reshape(Bn, T, B_HEADS, B_DV)
    beta = jax.nn.sigmoid(b_raw.astype(jnp.float32))
    g = -jnp.exp(a_log.astype(jnp.float32)) * jax.nn.softplus(
        a_raw.astype(jnp.float32) + dt_bias.astype(jnp.float32))
    o = gated_delta_rule(q, k, v, g, beta)
    o = rms_norm(o, g_o) * jax.nn.silu(z.reshape(Bn, T, B_HEADS, B_DV).astype(jnp.float32))
    return o.reshape(Bn, T, V_WIDTH).astype(dt)


def setup_inputs(seed: int = 0) -> dict:
    key = jax.random.key(seed)
    ks = jax.random.split(key, 24)
    nrm = jax.random.normal
    L, D, F = DEPTH, D_MODEL, FFN_HIDDEN
    dt_min, dt_max = 1e-3, 1e-1
    dt0 = jnp.exp(jax.random.uniform(ks[10], (L, B_HEADS)) * (np.log(dt_max) - np.log(dt_min)) + np.log(dt_min))
    return {
        "x": nrm(ks[0], (BATCH, SEQ, D), jnp.float32),
        "c": nrm(ks[1], (BATCH, D), jnp.float32),
        "ada_w": nrm(ks[2], (L, D, 6 * D), jnp.float32) * (0.5 * D ** -0.5),
        "ada_b": nrm(ks[3], (L, 6 * D), jnp.float32) * 0.02,
        "norm1_g": 1.0 + 0.02 * nrm(ks[4], (L, D), jnp.float32),
        "w_in": nrm(ks[5], (L, D, IN_TOTAL), jnp.float32) * D ** -0.5,
        "conv_w": nrm(ks[6], (L, B_CONV, IN_QKV), jnp.float32) * B_CONV ** -0.5,
        "spatial_w": nrm(ks[7], (L, A_GROUPS, A_CHUNK, A_CHUNK), jnp.float32) * A_CHUNK ** -0.5,
        "spatial_b": 1.0 + 0.02 * nrm(ks[8], (L, A_GROUPS, A_CHUNK), jnp.float32),
        "v_norm_g": 1.0 + 0.02 * nrm(ks[9], (L, A_WIDTH), jnp.float32),
        "a_log": jnp.log(jax.random.uniform(ks[11], (L, B_HEADS), jnp.float32, 1.0, 16.0)),
        "dt_bias": (dt0 + jnp.log(-jnp.expm1(-dt0))).astype(jnp.float32),
        "o_norm_g": 1.0 + 0.02 * nrm(ks[12], (L, B_DV), jnp.float32),
        "w_branch_a": nrm(ks[13], (L, A_WIDTH, D), jnp.float32) * A_WIDTH ** -0.5,
        "w_branch_b": nrm(ks[14], (L, V_WIDTH, D), jnp.float32) * V_WIDTH ** -0.5,
        "w_out": nrm(ks[15], (L, D, D), jnp.float32) * D ** -0.5,
        "norm2_g": 1.0 + 0.02 * nrm(ks[16], (L, D), jnp.float32),
        "w_ffn_in": nrm(ks[17], (L, D, 2 * F), jnp.float32) * D ** -0.5,
        "w_ffn_out": nrm(ks[18], (L, F, D), jnp.float32) * F ** -0.5,
        "final_g": 1.0 + 0.02 * nrm(ks[19], (D,), jnp.float32),
    }


def reference(x, c, ada_w, ada_b, norm1_g, w_in, conv_w, spatial_w, spatial_b, v_norm_g,
              a_log, dt_bias, o_norm_g, w_branch_a, w_branch_b, w_out, norm2_g,
              w_ffn_in, w_ffn_out, final_g):
    cond = jax.nn.silu(c)
    offs = [IN_UV, IN_UV + IN_QKV, IN_UV + IN_QKV + V_WIDTH,
            IN_UV + IN_QKV + V_WIDTH + B_HEADS, IN_UV + IN_QKV + V_WIDTH + 2 * B_HEADS]
    for i in range(DEPTH):
        mod = cond @ ada_w[i] + ada_b[i]
        sh1, sc1, gt1, sh2, sc2, gt2 = [m[:, None, :] for m in jnp.split(mod, 6, axis=-1)]
        h = rms_norm(x, norm1_g[i]) * (1 + sc1) + sh1
        proj = h @ w_in[i]
        uv, qkv, z, b_raw, a_raw, gates = jnp.split(proj, offs, axis=-1)
        y_a = chunked_spatial_gating(jax.nn.gelu(uv), spatial_w[i], spatial_b[i], v_norm_g[i])
        y_b = gated_deltanet(qkv, z, b_raw, a_raw, conv_w[i], a_log[i], dt_bias[i], o_norm_g[i])
        g_a, g_b = jnp.split(jax.nn.sigmoid(gates), 2, axis=-1)
        merged = g_a * (y_a @ w_branch_a[i]) + g_b * (y_b @ w_branch_b[i])
        x = x + gt1 * (merged @ w_out[i])
        h = rms_norm(x, norm2_g[i]) * (1 + sc2) + sh2
        gate, up = jnp.split(h @ w_ffn_in[i], 2, axis=-1)
        x = x + gt2 * ((jax.nn.silu(gate) * up) @ w_ffn_out[i])
    return rms_norm(x, final_g)
```

```python
import functools

import jax
import jax.numpy as jnp
from jax import lax
from jax.experimental import pallas as pl
from jax.experimental.pallas import tpu as pltpu

NORM_EPS = 1e-6
A_GROUPS = 8
A_CHUNK = 128
HEADS = 8
HEAD_DIM = 128
CONV_K = 4
GDN_CHUNK = 128
INV_BLOCK = 16
CONV_PAD = 8

V7X_VMEM_LIMIT = 56 * 1024 * 1024

TM_IN = 256
TM_OUT = 256

BF16 = jnp.bfloat16
F32 = jnp.float32


def _dot(a, b):
    return jnp.dot(a, b, preferred_element_type=F32)


def _dot_nt(a, b):
    return lax.dot_general(a, b, (((1,), (1,)), ((), ())), preferred_element_type=F32)


def _dot_tn(a, b):
    return lax.dot_general(a, b, (((0,), (0,)), ((), ())), preferred_element_type=F32)


def _softplus(x):
    return jnp.maximum(x, 0.0) + jnp.log1p(jnp.exp(-jnp.abs(x)))


def _gelu_tanh(x):
    return 0.5 * x * (1.0 + jnp.tanh(0.7978845608028654 * (x + 0.044715 * (x * x * x))))


def _silu(x):
    return x * jax.nn.sigmoid(x)


def _const_spec(shape):
    nd = len(shape)
    return pl.BlockSpec(shape, lambda *_: (0,) * nd, pipeline_mode=pl.Buffered(1))


def _ada_body(c_ref, w_ref, b_ref, o_ref):
    cond = _silu(c_ref[...]).astype(BF16)
    o_ref[0] = _dot(cond, w_ref[0].astype(BF16)) + b_ref[0]


def _ada_mod(c, ada_w, ada_b):
    L, D, D6 = ada_w.shape
    B = c.shape[0]
    nblk = D6 // D
    return pl.pallas_call(
        _ada_body,
        grid=(L, nblk),
        in_specs=[
            pl.BlockSpec((B, D), lambda l, j: (0, 0)),
            pl.BlockSpec((1, D, D), lambda l, j: (l, 0, j)),
            pl.BlockSpec((1, 1, D), lambda l, j: (l, 0, j)),
        ],
        out_specs=pl.BlockSpec((1, B, D), lambda l, j: (l, 0, j)),
        out_shape=jax.ShapeDtypeStruct((L, B, D6), F32),
        name="ada_mod",
    )(c, ada_w, ada_b.reshape(L, 1, D6))


def _in_proj_body(x_ref, mod_ref, n1g_ref, wuv_ref, wqkv_ref, wz_ref, wab_ref, wabt_ref,
                  wg_ref, convw_ref, sw_ref, sbt_ref, vg_ref, alog16_ref, dtb16_ref,
                  alogc_ref, dtbc_ref, wa_ref,
                  apart_ref, gb_ref, q_ref, k_ref, v_ref, zs_ref, gcol_ref, grow_ref,
                  conv_scr, ya_scr):
    tm, d = x_ref.shape[1], x_ref.shape[2]
    t = pl.program_id(1)

    x = x_ref[0]
    ms = jnp.mean(x * x, axis=-1, keepdims=True)
    h = x * lax.rsqrt(ms + NORM_EPS) * n1g_ref[...]
    h = h * (1.0 + mod_ref[0, 1:2, :]) + mod_ref[0, 0:1, :]
    hb = h.astype(BF16)

    u = _gelu_tanh(_dot(hb, wuv_ref[:, 0:d]))
    v = _gelu_tanh(_dot(hb, wuv_ref[:, d:2 * d]))
    row = lax.broadcasted_iota(jnp.int32, (A_CHUNK, A_CHUNK), 0)
    col = lax.broadcasted_iota(jnp.int32, (A_CHUNK, A_CHUNK), 1)
    causal = col <= row
    gd = d // A_GROUPS
    for g in range(A_GROUPS):
        cs = slice(g * gd, (g + 1) * gd)
        vg = v[:, cs]
        vn = vg * lax.rsqrt(jnp.mean(vg * vg, axis=-1, keepdims=True) + NORM_EPS) * vg_ref[:, cs]
        vn = vn.astype(BF16)
        wm = jnp.where(causal, sw_ref[g], 0.0).astype(BF16)
        bias = sbt_ref[:, g:g + 1]
        for n in range(tm // A_CHUNK):
            rs = slice(n * A_CHUNK, (n + 1) * A_CHUNK)
            s = _dot(wm, vn[rs, :]) + bias
            ya_scr[rs, cs] = (u[rs, cs] * s).astype(BF16)

    gates = jax.nn.sigmoid(_dot(hb, wg_ref[...]))
    gb_ref[0] = gates[:, d:2 * d].astype(BF16)
    apart_ref[0] = (gates[:, 0:d] * _dot(ya_scr[...], wa_ref[...])).astype(BF16)

    @pl.when(t == 0)
    def _():
        conv_scr[0:CONV_PAD, :] = jnp.zeros((CONV_PAD, conv_scr.shape[1]), F32)

    conv_scr[CONV_PAD:CONV_PAD + tm, :] = _dot(hb, wqkv_ref[...])
    acc = conv_scr[CONV_PAD:CONV_PAD + tm, :] * convw_ref[CONV_K - 1:CONV_K, :]
    for j in range(CONV_K - 1):
        off = CONV_PAD - (CONV_K - 1) + j
        acc = acc + conv_scr[off:off + tm, :] * convw_ref[j:j + 1, :]
    conv_scr[0:CONV_PAD, :] = conv_scr[tm:tm + CONV_PAD, :]
    act = _silu(acc)
    hd = HEAD_DIM
    for hh in range(HEADS):
        qs = slice(hh * hd, (hh + 1) * hd)
        ks = slice(d + hh * hd, d + (hh + 1) * hd)
        qh = act[:, qs]
        kh = act[:, ks]
        qn = qh * (lax.rsqrt(jnp.sum(qh * qh, axis=-1, keepdims=True) + NORM_EPS) * hd ** -0.5)
        kn = kh * lax.rsqrt(jnp.sum(kh * kh, axis=-1, keepdims=True) + NORM_EPS)
        q_ref[0, :, qs] = qn.astype(BF16)
        k_ref[0, :, qs] = kn.astype(BF16)
    v_ref[0] = act[:, 2 * d:3 * d].astype(BF16)

    zs_ref[0] = _silu(_dot(hb, wz_ref[...])).astype(BF16)

    ab = _dot(hb, wab_ref[...])
    lane = lax.broadcasted_iota(jnp.int32, ab.shape, 1)
    is_beta = lane < HEADS
    g_col = -jnp.exp(alog16_ref[...]) * _softplus(ab + dtb16_ref[...])
    g_col = jnp.where(is_beta, 0.0, g_col)
    ri = lax.broadcasted_iota(jnp.int32, (tm, tm), 0)
    ci = lax.broadcasted_iota(jnp.int32, (tm, tm), 1)
    same = (ri // GDN_CHUNK) == (ci // GDN_CHUNK)
    lower = jnp.where(same & (ci <= ri), 1.0, 0.0)
    upper = jnp.where(same & (ri <= ci), 1.0, 0.0)
    gam_col = jnp.dot(lower, g_col, preferred_element_type=F32, precision=lax.Precision.HIGHEST)
    gcol_ref[0] = jnp.where(is_beta, jax.nn.sigmoid(ab), gam_col)

    abt = _dot_nt(wabt_ref[...], hb)
    g_row = -jnp.exp(alogc_ref[...]) * _softplus(abt[HEADS:2 * HEADS, :] + dtbc_ref[...])
    grow_ref[0] = jnp.dot(g_row, upper, preferred_element_type=F32,
                          precision=lax.Precision.HIGHEST)


def _in_proj(x, mod, n1g, wuv, wqkv, wz, wab, wabt, wg, convw, sw, sbt, vg, alog16, dtb16,
             alogc, dtbc, wa):
    B, T, D = x.shape
    tm = TM_IN
    tok = lambda b, t: (b, t, 0)
    bf_out = jax.ShapeDtypeStruct((B, T, D), BF16)
    consts = [n1g, wuv, wqkv, wz, wab, wabt, wg, convw, sw, sbt, vg, alog16, dtb16, alogc,
              dtbc, wa]
    return pl.pallas_call(
        _in_proj_body,
        grid=(B, T // tm),
        in_specs=[pl.BlockSpec((1, tm, D), tok),
                  pl.BlockSpec((1, 6, D), lambda b, t: (b, 0, 0))]
        + [_const_spec(a.shape) for a in consts],
        out_specs=[pl.BlockSpec((1, tm, D), tok)] * 6
        + [pl.BlockSpec((1, tm, 2 * HEADS), tok),
           pl.BlockSpec((1, HEADS, tm), lambda b, t: (b, 0, t))],
        out_shape=[bf_out] * 6
        + [jax.ShapeDtypeStruct((B, T, 2 * HEADS), F32),
           jax.ShapeDtypeStruct((B, HEADS, T), F32)],
        scratch_shapes=[pltpu.VMEM((tm + CONV_PAD, 3 * D), F32),
                        pltpu.VMEM((tm, D), BF16)],
        compiler_params=pltpu.CompilerParams(
            dimension_semantics=("arbitrary", "arbitrary"),
            vmem_limit_bytes=V7X_VMEM_LIMIT),
        name="in_proj",
    )(x, mod, *consts)


def _gdn_body(q_ref, k_ref, v_ref, zs_ref, gcol_ref, grow_ref, og_ref, y_ref, s_scr):
    c = q_ref.shape[1]
    hd = HEAD_DIM
    t = pl.program_id(1)

    @pl.when(t == 0)
    def _():
        s_scr[...] = jnp.zeros(s_scr.shape, F32)

    gc = gcol_ref[0]
    gr = grow_ref[0]
    row = lax.broadcasted_iota(jnp.int32, (c, c), 0)
    col = lax.broadcasted_iota(jnp.int32, (c, c), 1)
    incl = col <= row
    strict = col < row
    nblk = c // INV_BLOCK
    sh = INV_BLOCK.bit_length() - 1
    lane_c = lax.broadcasted_iota(jnp.int32, (INV_BLOCK, c), 1)
    row_c = lax.broadcasted_iota(jnp.int32, (INV_BLOCK, c), 0)
    lane_blk = lane_c >> sh
    lane_in = lane_c & (INV_BLOCK - 1)
    spread = jnp.where((row >> sh) == (col >> sh), 1.0, 0.0).astype(BF16)
    eye_c = jnp.where(lane_in == row_c, 1.0, 0.0)
    merge_masks = []
    s = INV_BLOCK
    while s < c:
        b = s.bit_length() - 1
        merge_masks.append((((row >> b) & 1) == 1) & ((col >> b) == (row >> b) - 1))
        s *= 2

    for hh in range(HEADS):
        cs = slice(hh * hd, (hh + 1) * hd)
        qh = q_ref[0, :, cs]
        kh = k_ref[0, :, cs]
        qf = qh.astype(F32)
        kf = kh.astype(F32)
        vf = v_ref[0, :, cs].astype(F32)
        beta = gc[:, hh:hh + 1]
        gam = gc[:, HEADS + hh:HEADS + hh + 1]
        gam_r = gr[hh:hh + 1, :]
        g_last = gc[c - 1:c, HEADS + hh:HEADS + hh + 1]
        dec = jnp.where(incl, jnp.exp(jnp.minimum(gam - gam_r, 0.0)), 0.0)
        e_gam = jnp.exp(gam)
        kb = kf * beta
        a_mat = jnp.where(strict, _dot_nt(kb.astype(BF16), kh) * dec, 0.0)
        a_diag = jnp.zeros((INV_BLOCK, c), F32)
        for blk in range(nblk):
            a_diag = a_diag + jnp.where(
                lane_blk == blk, a_mat[blk * INV_BLOCK:(blk + 1) * INV_BLOCK, :], 0.0)
        picked = jnp.concatenate(
            [jnp.where(lane_in == j, a_diag, 0.0) for j in range(INV_BLOCK)], axis=0)
        coef = _dot(picked.astype(BF16), spread)
        x_inv = eye_c
        for j in range(INV_BLOCK - 1):
            x_inv = x_inv - coef[j * INV_BLOCK:(j + 1) * INV_BLOCK, :] * x_inv[j:j + 1, :]
        t_mat = jnp.concatenate(
            [jnp.where(lane_blk == blk, x_inv, 0.0) for blk in range(nblk)], axis=0)
        for mask in merge_masks:
            a_off = jnp.where(mask, a_mat, 0.0).astype(BF16)
            tb = t_mat.astype(BF16)
            t_mat = t_mat - _dot(tb, _dot(a_off, tb).astype(BF16))
        rhs = jnp.concatenate([vf * beta, kb * e_gam], axis=1).astype(BF16)
        uw = _dot(t_mat.astype(BF16), rhs)
        qk = _dot_nt(qh, kh) * dec
        q_dec = qf * e_gam
        k_dec = kf * jnp.exp(g_last - gam)
        s_old = s_scr[hh]
        ws = _dot(jnp.concatenate([uw[:, hd:], q_dec], axis=0).astype(BF16),
                  s_old.astype(BF16))
        v_new = (uw[:, :hd] - ws[:c]).astype(BF16)
        o = ws[c:] + _dot(qk.astype(BF16), v_new)
        s_scr[hh] = s_old * jnp.exp(g_last) + _dot_tn(k_dec.astype(BF16), v_new)
        on = o * lax.rsqrt(jnp.mean(o * o, axis=-1, keepdims=True) + NORM_EPS) * og_ref[...]
        y_ref[0, :, cs] = (on * zs_ref[0, :, cs].astype(F32)).astype(BF16)


def _gdn(q, k, v, zs, gcol, grow, og):
    B, T, D = q.shape
    c = GDN_CHUNK
    tok = lambda b, t: (b, t, 0)
    return pl.pallas_call(
        _gdn_body,
        grid=(B, T // c),
        in_specs=[pl.BlockSpec((1, c, D), tok)] * 4
        + [pl.BlockSpec((1, c, 2 * HEADS), tok),
           pl.BlockSpec((1, HEADS, c), lambda b, t: (b, 0, t)),
           _const_spec(og.shape)],
        out_specs=pl.BlockSpec((1, c, D), tok),
        out_shape=jax.ShapeDtypeStruct((B, T, D), BF16),
        scratch_shapes=[pltpu.VMEM((HEADS, HEAD_DIM, HEAD_DIM), F32)],
        compiler_params=pltpu.CompilerParams(
            dimension_semantics=("arbitrary", "arbitrary")),
        name="gdn",
    )(q, k, v, zs, gcol, grow, og)


def _out_ffn_body(x_ref, apart_ref, gb_ref, yb_ref, mod_ref, wb_ref, wo_ref, n2g_ref,
                  wfi_ref, wfo_ref, fg_ref, o_ref, *, final_norm, ffn_split):
    f = wfo_ref.shape[0]
    x = x_ref[0]
    merged = apart_ref[0].astype(F32) + gb_ref[0].astype(F32) * _dot(yb_ref[0], wb_ref[...])
    x = x + mod_ref[0, 2:3, :] * _dot(merged.astype(BF16), wo_ref[...])
    ms = jnp.mean(x * x, axis=-1, keepdims=True)
    h = x * lax.rsqrt(ms + NORM_EPS) * n2g_ref[...]
    h = h * (1.0 + mod_ref[0, 4:5, :]) + mod_ref[0, 3:4, :]
    hb = h.astype(BF16)
    fc = f // ffn_split
    acc = None
    for j in range(ffn_split):
        gate = _dot(hb, wfi_ref[:, j * fc:(j + 1) * fc])
        up = _dot(hb, wfi_ref[:, f + j * fc:f + (j + 1) * fc])
        part = _dot((_silu(gate) * up).astype(BF16), wfo_ref[j * fc:(j + 1) * fc, :])
        acc = part if acc is None else acc + part
    x = x + mod_ref[0, 5:6, :] * acc
    if final_norm:
        ms = jnp.mean(x * x, axis=-1, keepdims=True)
        x = x * lax.rsqrt(ms + NORM_EPS) * fg_ref[...]
    o_ref[0] = x


def _out_ffn(x, apart, gb, yb, mod, wb, wo, n2g, wfi, wfo, fg, final_norm):
    B, T, D = x.shape
    tm = TM_OUT
    tok = lambda b, t: (b, t, 0)
    consts = [wb, wo, n2g, wfi, wfo, fg]
    return pl.pallas_call(
        functools.partial(_out_ffn_body, final_norm=final_norm, ffn_split=2),
        grid=(B, T // tm),
        in_specs=[pl.BlockSpec((1, tm, D), tok)] * 4
        + [pl.BlockSpec((1, 6, D), lambda b, t: (b, 0, 0))]
        + [_const_spec(a.shape) for a in consts],
        out_specs=pl.BlockSpec((1, tm, D), tok),
        out_shape=jax.ShapeDtypeStruct((B, T, D), F32),
        compiler_params=pltpu.CompilerParams(
            dimension_semantics=("arbitrary", "arbitrary"),
            vmem_limit_bytes=V7X_VMEM_LIMIT),
        name="out_ffn",
    )(x, apart, gb, yb, mod, *consts)


def kernel(x, c, ada_w, ada_b, norm1_g, w_in, conv_w, spatial_w, spatial_b, v_norm_g, a_log,
           dt_bias, o_norm_g, w_branch_a, w_branch_b, w_out, norm2_g, w_ffn_in, w_ffn_out,
           final_g):
    B, T, D = x.shape
    L = ada_w.shape[0]
    assert D == A_GROUPS * A_CHUNK == HEADS * HEAD_DIM
    assert T % TM_IN == 0 and T % TM_OUT == 0 and TM_IN % GDN_CHUNK == 0
    assert conv_w.shape[1] == CONV_K

    mod_all = _ada_mod(c, ada_w, ada_b).reshape(L, B, 6, D)
    o_uv, o_qkv, o_z, o_ab, o_g = 0, 2 * D, 5 * D, 6 * D, 6 * D + 2 * HEADS
    zeros8 = jnp.zeros((HEADS,), F32)
    for i in range(L):
        wi = w_in[i]
        wab = wi[:, o_ab:o_g].astype(BF16)
        apart, gb, q, k, v, zs, gcol, grow = _in_proj(
            x, mod_all[i], norm1_g[i].reshape(1, D),
            wi[:, o_uv:o_qkv].astype(BF16), wi[:, o_qkv:o_z].astype(BF16),
            wi[:, o_z:o_ab].astype(BF16), wab, wab.T, wi[:, o_g:].astype(BF16),
            conv_w[i], spatial_w[i], spatial_b[i].T, v_norm_g[i].reshape(1, D),
            jnp.concatenate([zeros8, a_log[i]]).reshape(1, 2 * HEADS),
            jnp.concatenate([zeros8, dt_bias[i]]).reshape(1, 2 * HEADS),
            a_log[i].reshape(HEADS, 1), dt_bias[i].reshape(HEADS, 1),
            w_branch_a[i].astype(BF16))
        yb = _gdn(q, k, v, zs, gcol, grow, o_norm_g[i].reshape(1, HEAD_DIM))
        x = _out_ffn(x, apart, gb, yb, mod_all[i], w_branch_b[i].astype(BF16),
                     w_out[i].astype(BF16), norm2_g[i].reshape(1, D),
                     w_ffn_in[i].astype(BF16), w_ffn_out[i].astype(BF16),
                     final_g.reshape(1, D), final_norm=(i == L - 1))
    return x
```

```python
import functools

import jax
import jax.numpy as jnp
from jax import lax
from jax.experimental import pallas as pl
from jax.experimental.pallas import tpu as pltpu

NORM_EPS = 1e-6
A_GROUPS = 8
A_CHUNK = 128
HEADS = 8
HEAD_DIM = 128
CONV_K = 4
GDN_CHUNK = 128
INV_BLOCK = 16
CONV_PAD = 8

V7X_VMEM_LIMIT = 56 * 1024 * 1024

TM_IN = 256
TM_OUT = 256

BF16 = jnp.bfloat16
F32 = jnp.float32


def _dot(a, b):
    return jnp.dot(a, b, preferred_element_type=F32)


def _dot_nt(a, b):
    return lax.dot_general(a, b, (((1,), (1,)), ((), ())), preferred_element_type=F32)


def _dot_tn(a, b):
    return lax.dot_general(a, b, (((0,), (0,)), ((), ())), preferred_element_type=F32)


def _softplus(x):
    return jnp.maximum(x, 0.0) + jnp.log1p(jnp.exp(-jnp.abs(x)))


def _gelu_tanh(x):
    return 0.5 * x * (1.0 + jnp.tanh(0.7978845608028654 * (x + 0.044715 * (x * x * x))))


def _silu(x):
    return x * jax.nn.sigmoid(x)


def _const_spec(shape):
    nd = len(shape)
    return pl.BlockSpec(shape, lambda *_: (0,) * nd, pipeline_mode=pl.Buffered(1))


def _ada_body(c_ref, w_ref, b_ref, o_ref):
    cond = _silu(c_ref[...]).astype(BF16)
    o_ref[0] = _dot(cond, w_ref[0].astype(BF16)) + b_ref[0]


def _ada_mod(c, ada_w, ada_b):
    L, D, D6 = ada_w.shape
    B = c.shape[0]
    nblk = D6 // D
    return pl.pallas_call(
        _ada_body,
        grid=(L, nblk),
        in_specs=[
            pl.BlockSpec((B, D), lambda l, j: (0, 0)),
            pl.BlockSpec((1, D, D), lambda l, j: (l, 0, j)),
            pl.BlockSpec((1, 1, D), lambda l, j: (l, 0, j)),
        ],
        out_specs=pl.BlockSpec((1, B, D), lambda l, j: (l, 0, j)),
        out_shape=jax.ShapeDtypeStruct((L, B, D6), F32),
        name="ada_mod",
    )(c, ada_w, ada_b.reshape(L, 1, D6))


def _in_proj_body(x_ref, mod_ref, n1g_ref, wuv_ref, wqkv_ref, wz_ref, wab_ref, wabt_ref,
                  wg_ref, convw_ref, sw_ref, sbt_ref, vg_ref, alog16_ref, dtb16_ref,
                  alogc_ref, dtbc_ref, wa_ref,
                  apart_ref, gb_ref, q_ref, k_ref, v_ref, zs_ref, gcol_ref, grow_ref,
                  conv_scr, ya_scr):
    tm, d = x_ref.shape[1], x_ref.shape[2]
    t = pl.program_id(1)

    x = x_ref[0]
    ms = jnp.mean(x * x, axis=-1, keepdims=True)
    h = x * lax.rsqrt(ms + NORM_EPS) * n1g_ref[...]
    h = h * (1.0 + mod_ref[0, 1:2, :]) + mod_ref[0, 0:1, :]
    hb = h.astype(BF16)

    u = _gelu_tanh(_dot(hb, wuv_ref[:, 0:d]))
    v = _gelu_tanh(_dot(hb, wuv_ref[:, d:2 * d]))
    row = lax.broadcasted_iota(jnp.int32, (A_CHUNK, A_CHUNK), 0)
    col = lax.broadcasted_iota(jnp.int32, (A_CHUNK, A_CHUNK), 1)
    causal = col <= row
    gd = d // A_GROUPS
    for g in range(A_GROUPS):
        cs = slice(g * gd, (g + 1) * gd)
        vg = v[:, cs]
        vn = vg * lax.rsqrt(jnp.mean(vg * vg, axis=-1, keepdims=True) + NORM_EPS) * vg_ref[:, cs]
        vn = vn.astype(BF16)
        wm = jnp.where(causal, sw_ref[g], 0.0).astype(BF16)
        bias = sbt_ref[:, g:g + 1]
        for n in range(tm // A_CHUNK):
            rs = slice(n * A_CHUNK, (n + 1) * A_CHUNK)
            s = _dot(wm, vn[rs, :]) + bias
            ya_scr[rs, cs] = (u[rs, cs] * s).astype(BF16)

    gates = jax.nn.sigmoid(_dot(hb, wg_ref[...]))
    gb_ref[0] = gates[:, d:2 * d].astype(BF16)
    apart_ref[0] = (gates[:, 0:d] * _dot(ya_scr[...], wa_ref[...])).astype(BF16)

    @pl.when(t == 0)
    def _():
        conv_scr[0:CONV_PAD, :] = jnp.zeros((CONV_PAD, conv_scr.shape[1]), F32)

    conv_scr[CONV_PAD:CONV_PAD + tm, :] = _dot(hb, wqkv_ref[...])
    acc = conv_scr[CONV_PAD:CONV_PAD + tm, :] * convw_ref[CONV_K - 1:CONV_K, :]
    for j in range(CONV_K - 1):
        off = CONV_PAD - (CONV_K - 1) + j
        acc = acc + conv_scr[off:off + tm, :] * convw_ref[j:j + 1, :]
    conv_scr[0:CONV_PAD, :] = conv_scr[tm:tm + CONV_PAD, :]
    act = _silu(acc)
    hd = HEAD_DIM
    for hh in range(HEADS):
        qs = slice(hh * hd, (hh + 1) * hd)
        ks = slice(d + hh * hd, d + (hh + 1) * hd)
        qh = act[:, qs]
        kh = act[:, ks]
        qn = qh * (lax.rsqrt(jnp.sum(qh * qh, axis=-1, keepdims=True) + NORM_EPS) * hd ** -0.5)
        kn = kh * lax.rsqrt(jnp.sum(kh * kh, axis=-1, keepdims=True) + NORM_EPS)
        q_ref[0, :, qs] = qn.astype(BF16)
        k_ref[0, :, qs] = kn.astype(BF16)
    v_ref[0] = act[:, 2 * d:3 * d].astype(BF16)

    zs_ref[0] = _silu(_dot(hb, wz_ref[...])).astype(BF16)

    ab = _dot(hb, wab_ref[...])
    lane = lax.broadcasted_iota(jnp.int32, ab.shape, 1)
    is_beta = lane < HEADS
    g_col = -jnp.exp(alog16_ref[...]) * _softplus(ab + dtb16_ref[...])
    g_col = jnp.where(is_beta, 0.0, g_col)
    ri = lax.broadcasted_iota(jnp.int32, (tm, tm), 0)
    ci = lax.broadcasted_iota(jnp.int32, (tm, tm), 1)
    same = (ri // GDN_CHUNK) == (ci // GDN_CHUNK)
    lower = jnp.where(same & (ci <= ri), 1.0, 0.0)
    upper = jnp.where(same & (ri <= ci), 1.0, 0.0)
    gam_col = jnp.dot(lower, g_col, preferred_element_type=F32, precision=lax.Precision.HIGHEST)
    gcol_ref[0] = jnp.where(is_beta, jax.nn.sigmoid(ab), gam_col)

    abt = _dot_nt(wabt_ref[...], hb)
    g_row = -jnp.exp(alogc_ref[...]) * _softplus(abt[HEADS:2 * HEADS, :] + dtbc_ref[...])
    grow_ref[0] = jnp.dot(g_row, upper, preferred_element_type=F32,
                          precision=lax.Precision.HIGHEST)


def _in_proj(x, mod, n1g, wuv, wqkv, wz, wab, wabt, wg, convw, sw, sbt, vg, alog16, dtb16,
             alogc, dtbc, wa):
    B, T, D = x.shape
    tm = TM_IN
    tok = lambda b, t: (b, t, 0)
    bf_out = jax.ShapeDtypeStruct((B, T, D), BF16)
    consts = [n1g, wuv, wqkv, wz, wab, wabt, wg, convw, sw, sbt, vg, alog16, dtb16, alogc,
              dtbc, wa]
    return pl.pallas_call(
        _in_proj_body,
        grid=(B, T // tm),
        in_specs=[pl.BlockSpec((1, tm, D), tok),
                  pl.BlockSpec((1, 6, D), lambda b, t: (b, 0, 0))]
        + [_const_spec(a.shape) for a in consts],
        out_specs=[pl.BlockSpec((1, tm, D), tok)] * 6
        + [pl.BlockSpec((1, tm, 2 * HEADS), tok),
           pl.BlockSpec((1, HEADS, tm), lambda b, t: (b, 0, t))],
        out_shape=[bf_out] * 6
        + [jax.ShapeDtypeStruct((B, T, 2 * HEADS), F32),
           jax.ShapeDtypeStruct((B, HEADS, T), F32)],
        scratch_shapes=[pltpu.VMEM((tm + CONV_PAD, 3 * D), F32),
                        pltpu.VMEM((tm, D), BF16)],
        compiler_params=pltpu.CompilerParams(
            dimension_semantics=("arbitrary", "arbitrary"),
            vmem_limit_bytes=V7X_VMEM_LIMIT),
        name="in_proj",
    )(x, mod, *consts)


def _gdn_body(q_ref, k_ref, v_ref, zs_ref, gcol_ref, grow_ref, og_ref, y_ref, s_scr):
    c = q_ref.shape[1]
    hd = HEAD_DIM
    t = pl.program_id(1)

    @pl.when(t == 0)
    def _():
        s_scr[...] = jnp.zeros(s_scr.shape, F32)

    gc = gcol_ref[0]
    gr = grow_ref[0]
    row = lax.broadcasted_iota(jnp.int32, (c, c), 0)
    col = lax.broadcasted_iota(jnp.int32, (c, c), 1)
    incl = col <= row
    strict = col < row
    nblk = c // INV_BLOCK
    sh = INV_BLOCK.bit_length() - 1
    lane_c = lax.broadcasted_iota(jnp.int32, (INV_BLOCK, c), 1)
    row_c = lax.broadcasted_iota(jnp.int32, (INV_BLOCK, c), 0)
    lane_blk = lane_c >> sh
    lane_in = lane_c & (INV_BLOCK - 1)
    spread = jnp.where((row >> sh) == (col >> sh), 1.0, 0.0).astype(BF16)
    eye_c = jnp.where(lane_in == row_c, 1.0, 0.0)
    merge_masks = []
    s = INV_BLOCK
    while s < c:
        b = s.bit_length() - 1
        merge_masks.append((((row >> b) & 1) == 1) & ((col >> b) == (row >> b) - 1))
        s *= 2

    hs = range(HEADS)
    sl = [slice(h * hd, (h + 1) * hd) for h in hs]
    qh = [q_ref[0, :, sl[h]] for h in hs]
    kh = [k_ref[0, :, sl[h]] for h in hs]
    beta = [gc[:, h:h + 1] for h in hs]
    gam = [gc[:, HEADS + h:HEADS + h + 1] for h in hs]
    g_last = [gc[c - 1:c, HEADS + h:HEADS + h + 1] for h in hs]
    kb = [kh[h].astype(F32) * beta[h] for h in hs]
    kk = [_dot_nt(kb[h].astype(BF16), kh[h]) for h in hs]
    qk = [_dot_nt(qh[h], kh[h]) for h in hs]
    dec = [jnp.where(incl, jnp.exp(jnp.minimum(gam[h] - gr[h:h + 1, :], 0.0)), 0.0) for h in hs]
    a_mat = [jnp.where(strict, kk[h] * dec[h], 0.0) for h in hs]
    qkd = [(qk[h] * dec[h]).astype(BF16) for h in hs]

    coef = []
    for h in hs:
        a_diag = jnp.zeros((INV_BLOCK, c), F32)
        for blk in range(nblk):
            a_diag = a_diag + jnp.where(
                lane_blk == blk, a_mat[h][blk * INV_BLOCK:(blk + 1) * INV_BLOCK, :], 0.0)
        picked = jnp.concatenate(
            [jnp.where(lane_in == j, a_diag, 0.0) for j in range(INV_BLOCK)], axis=0)
        coef.append(_dot(picked.astype(BF16), spread))
    x_inv = [eye_c for _ in hs]
    for j in range(INV_BLOCK - 1):
        x_inv = [x_inv[h] - coef[h][j * INV_BLOCK:(j + 1) * INV_BLOCK, :] * x_inv[h][j:j + 1, :]
                 for h in hs]
    t_mat = [jnp.concatenate([jnp.where(lane_blk == blk, x_inv[h], 0.0) for blk in range(nblk)],
                             axis=0) for h in hs]
    for mask in merge_masks:
        tb = [t_mat[h].astype(BF16) for h in hs]
        inner = [_dot(jnp.where(mask, a_mat[h], 0.0).astype(BF16), tb[h]).astype(BF16) for h in hs]
        t_mat = [t_mat[h] - _dot(tb[h], inner[h]) for h in hs]

    e_gam = [jnp.exp(gam[h]) for h in hs]
    rhs = [jnp.concatenate([v_ref[0, :, sl[h]].astype(F32) * beta[h], kb[h] * e_gam[h]],
                           axis=1).astype(BF16) for h in hs]
    uw = [_dot(t_mat[h].astype(BF16), rhs[h]) for h in hs]
    s_old = [s_scr[h] for h in hs]
    ws = [_dot(jnp.concatenate([uw[h][:, hd:], qh[h].astype(F32) * e_gam[h]],
                               axis=0).astype(BF16), s_old[h].astype(BF16))
          for h in hs]
    v_new = [(uw[h][:, :hd] - ws[h][:c]).astype(BF16) for h in hs]
    k_dec = [(kh[h].astype(F32) * jnp.exp(g_last[h] - gam[h])).astype(BF16) for h in hs]
    o = [ws[h][c:] + _dot(qkd[h], v_new[h]) for h in hs]
    s_new = [s_old[h] * jnp.exp(g_last[h]) + _dot_tn(k_dec[h], v_new[h]) for h in hs]
    for h in hs:
        s_scr[h] = s_new[h]
        on = o[h] * lax.rsqrt(jnp.mean(o[h] * o[h], axis=-1, keepdims=True) + NORM_EPS)
        y_ref[0, :, sl[h]] = (on * og_ref[...] * zs_ref[0, :, sl[h]].astype(F32)).astype(BF16)


def _gdn(q, k, v, zs, gcol, grow, og):
    B, T, D = q.shape
    c = GDN_CHUNK
    tok = lambda b, t: (b, t, 0)
    return pl.pallas_call(
        _gdn_body,
        grid=(B, T // c),
        in_specs=[pl.BlockSpec((1, c, D), tok)] * 4
        + [pl.BlockSpec((1, c, 2 * HEADS), tok),
           pl.BlockSpec((1, HEADS, c), lambda b, t: (b, 0, t)),
           _const_spec(og.shape)],
        out_specs=pl.BlockSpec((1, c, D), tok),
        out_shape=jax.ShapeDtypeStruct((B, T, D), BF16),
        scratch_shapes=[pltpu.VMEM((HEADS, HEAD_DIM, HEAD_DIM), F32)],
        compiler_params=pltpu.CompilerParams(
            dimension_semantics=("arbitrary", "arbitrary")),
        name="gdn",
    )(q, k, v, zs, gcol, grow, og)


def _out_ffn_body(x_ref, apart_ref, gb_ref, yb_ref, mod_ref, wb_ref, wo_ref, n2g_ref,
                  wfi_ref, wfo_ref, fg_ref, o_ref, *, final_norm, ffn_split):
    f = wfo_ref.shape[0]
    x = x_ref[0]
    merged = apart_ref[0].astype(F32) + gb_ref[0].astype(F32) * _dot(yb_ref[0], wb_ref[...])
    x = x + mod_ref[0, 2:3, :] * _dot(merged.astype(BF16), wo_ref[...])
    ms = jnp.mean(x * x, axis=-1, keepdims=True)
    h = x * lax.rsqrt(ms + NORM_EPS) * n2g_ref[...]
    h = h * (1.0 + mod_ref[0, 4:5, :]) + mod_ref[0, 3:4, :]
    hb = h.astype(BF16)
    fc = f // ffn_split
    acc = None
    for j in range(ffn_split):
        gate = _dot(hb, wfi_ref[:, j * fc:(j + 1) * fc])
        up = _dot(hb, wfi_ref[:, f + j * fc:f + (j + 1) * fc])
        part = _dot((_silu(gate) * up).astype(BF16), wfo_ref[j * fc:(j + 1) * fc, :])
        acc = part if acc is None else acc + part
    x = x + mod_ref[0, 5:6, :] * acc
    if final_norm:
        ms = jnp.mean(x * x, axis=-1, keepdims=True)
        x = x * lax.rsqrt(ms + NORM_EPS) * fg_ref[...]
    o_ref[0] = x


def _out_ffn(x, apart, gb, yb, mod, wb, wo, n2g, wfi, wfo, fg, final_norm):
    B, T, D = x.shape
    tm = TM_OUT
    tok = lambda b, t: (b, t, 0)
    consts = [wb, wo, n2g, wfi, wfo, fg]
    return pl.pallas_call(
        functools.partial(_out_ffn_body, final_norm=final_norm, ffn_split=2),
        grid=(B, T // tm),
        in_specs=[pl.BlockSpec((1, tm, D), tok)] * 4
        + [pl.BlockSpec((1, 6, D), lambda b, t: (b, 0, 0))]
        + [_const_spec(a.shape) for a in consts],
        out_specs=pl.BlockSpec((1, tm, D), tok),
        out_shape=jax.ShapeDtypeStruct((B, T, D), F32),
        compiler_params=pltpu.CompilerParams(
            dimension_semantics=("arbitrary", "arbitrary"),
            vmem_limit_bytes=V7X_VMEM_LIMIT),
        name="out_ffn",
    )(x, apart, gb, yb, mod, *consts)


def kernel(x, c, ada_w, ada_b, norm1_g, w_in, conv_w, spatial_w, spatial_b, v_norm_g, a_log,
           dt_bias, o_norm_g, w_branch_a, w_branch_b, w_out, norm2_g, w_ffn_in, w_ffn_out,
           final_g):
    B, T, D = x.shape
    L = ada_w.shape[0]
    assert D == A_GROUPS * A_CHUNK == HEADS * HEAD_DIM
    assert T % TM_IN == 0 and T % TM_OUT == 0 and TM_IN % GDN_CHUNK == 0
    assert conv_w.shape[1] == CONV_K

    mod_all = _ada_mod(c, ada_w, ada_b).reshape(L, B, 6, D)
    o_uv, o_qkv, o_z, o_ab, o_g = 0, 2 * D, 5 * D, 6 * D, 6 * D + 2 * HEADS
    zeros8 = jnp.zeros((HEADS,), F32)
    for i in range(L):
        wi = w_in[i]
        wab = wi[:, o_ab:o_g].astype(BF16)
        apart, gb, q, k, v, zs, gcol, grow = _in_proj(
            x, mod_all[i], norm1_g[i].reshape(1, D),
            wi[:, o_uv:o_qkv].astype(BF16), wi[:, o_qkv:o_z].astype(BF16),
            wi[:, o_z:o_ab].astype(BF16), wab, wab.T, wi[:, o_g:].astype(BF16),
            conv_w[i], spatial_w[i], spatial_b[i].T, v_norm_g[i].reshape(1, D),
            jnp.concatenate([zeros8, a_log[i]]).reshape(1, 2 * HEADS),
            jnp.concatenate([zeros8, dt_bias[i]]).reshape(1, 2 * HEADS),
            a_log[i].reshape(HEADS, 1), dt_bias[i].reshape(HEADS, 1),
            w_branch_a[i].astype(BF16))
        yb = _gdn(q, k, v, zs, gcol, grow, o_norm_g[i].reshape(1, HEAD_DIM))
        x = _out_ffn(x, apart, gb, yb, mod_all[i], w_branch_b[i].astype(BF16),
                     w_out[i].astype(BF16), norm2_g[i].reshape(1, D),
                     w_ffn_in[i].astype(BF16), w_ffn_out[i].astype(BF16),
                     final_g.reshape(1, D), final_norm=(i == L - 1))
    return x
```

```python
import functools

import jax
import jax.numpy as jnp
from jax import lax
from jax.experimental import pallas as pl
from jax.experimental.pallas import tpu as pltpu

NORM_EPS = 1e-6
A_GROUPS = 8
A_CHUNK = 128
HEADS = 8
HEAD_DIM = 128
CONV_K = 4
GDN_CHUNK = 128
INV_BLOCK = 16
CONV_PAD = 8

V7X_VMEM_LIMIT = 56 * 1024 * 1024

TM_IN = 512
TM_OUT = 512
GDN_STEP_CHUNKS = 2
FFN_SPLIT = 11

BF16 = jnp.bfloat16
F32 = jnp.float32


def _dot(a, b):
    return jnp.dot(a, b, preferred_element_type=F32)


def _dot_nt(a, b):
    return lax.dot_general(a, b, (((1,), (1,)), ((), ())), preferred_element_type=F32)


def _dot_tn(a, b):
    return lax.dot_general(a, b, (((0,), (0,)), ((), ())), preferred_element_type=F32)


def _softplus(x):
    return jnp.maximum(x, 0.0) + jnp.log1p(jnp.exp(-jnp.abs(x)))


def _gelu_tanh(x):
    return 0.5 * x * (1.0 + jnp.tanh(0.7978845608028654 * (x + 0.044715 * (x * x * x))))


def _silu(x):
    return x * jax.nn.sigmoid(x)


def _const_spec(shape):
    nd = len(shape)
    return pl.BlockSpec(shape, lambda *_: (0,) * nd, pipeline_mode=pl.Buffered(1))


def _ada_body(c_ref, w_ref, b_ref, o_ref):
    cond = _silu(c_ref[...]).astype(BF16)
    o_ref[0] = _dot(cond, w_ref[0].astype(BF16)) + b_ref[0]


def _ada_mod(c, ada_w, ada_b):
    L, D, D6 = ada_w.shape
    B = c.shape[0]
    nblk = D6 // D
    return pl.pallas_call(
        _ada_body,
        grid=(L, nblk),
        in_specs=[
            pl.BlockSpec((B, D), lambda l, j: (0, 0)),
            pl.BlockSpec((1, D, D), lambda l, j: (l, 0, j)),
            pl.BlockSpec((1, 1, D), lambda l, j: (l, 0, j)),
        ],
        out_specs=pl.BlockSpec((1, B, D), lambda l, j: (l, 0, j)),
        out_shape=jax.ShapeDtypeStruct((L, B, D6), F32),
        name="ada_mod",
    )(c, ada_w, ada_b.reshape(L, 1, D6))


def _in_proj_body(x_ref, mod_ref, n1g_ref, wuv_ref, wqkv_ref, wz_ref, wab_ref, wabt_ref,
                  wg_ref, convw_ref, sw_ref, sbt_ref, vg_ref, alog16_ref, dtb16_ref,
                  alogc_ref, dtbc_ref, wa_ref,
                  apart_ref, gb_ref, q_ref, k_ref, v_ref, zs_ref, gcol_ref, grow_ref,
                  conv_scr, carry_scr, ya_scr):
    tm, d = x_ref.shape[1], x_ref.shape[2]
    t = pl.program_id(1)

    @pl.when(t == 0)
    def _():
        carry_scr[...] = jnp.zeros(carry_scr.shape, F32)

    x = x_ref[0]
    ms = jnp.mean(x * x, axis=-1, keepdims=True)
    h = x * lax.rsqrt(ms + NORM_EPS) * n1g_ref[...]
    h = h * (1.0 + mod_ref[0, 1:2, :]) + mod_ref[0, 0:1, :]
    hb = h.astype(BF16)

    hd = HEAD_DIM

    def conv_stage(blk, pre):
        bs = slice(blk * d, (blk + 1) * d)
        conv_scr[blk, 0:CONV_PAD, :] = carry_scr[:, bs]
        conv_scr[blk, CONV_PAD:CONV_PAD + tm, :] = pre
        carry_scr[:, bs] = conv_scr[blk, tm:tm + CONV_PAD, :]
        acc = pre * convw_ref[CONV_K - 1:CONV_K, bs]
        for j in range(CONV_K - 1):
            off = CONV_PAD - (CONV_K - 1) + j
            acc = acc + conv_scr[blk, off:off + tm, :] * convw_ref[j:j + 1, bs]
        return _silu(acc)

    def l2_heads(act, out_ref, scale):
        for hh in range(HEADS):
            cs = slice(hh * hd, (hh + 1) * hd)
            ah = act[:, cs]
            inv = lax.rsqrt(jnp.sum(ah * ah, axis=-1, keepdims=True) + NORM_EPS) * scale
            out_ref[0, :, cs] = (ah * inv).astype(BF16)

    d_u = _dot(hb, wuv_ref[:, 0:d])
    d_v = _dot(hb, wuv_ref[:, d:2 * d])
    u = _gelu_tanh(d_u)
    d_ga = _dot(hb, wg_ref[:, 0:d])
    v = _gelu_tanh(d_v)
    d_gb = _dot(hb, wg_ref[:, d:2 * d])

    row = lax.broadcasted_iota(jnp.int32, (A_CHUNK, A_CHUNK), 0)
    col = lax.broadcasted_iota(jnp.int32, (A_CHUNK, A_CHUNK), 1)
    causal = col <= row
    gd = d // A_GROUPS
    for g in range(A_GROUPS):
        cs = slice(g * gd, (g + 1) * gd)
        vg = v[:, cs]
        vn = vg * lax.rsqrt(jnp.mean(vg * vg, axis=-1, keepdims=True) + NORM_EPS) * vg_ref[:, cs]
        vn = vn.astype(BF16)
        wm = jnp.where(causal, sw_ref[g], 0.0).astype(BF16)
        bias = sbt_ref[:, g:g + 1]
        for n in range(tm // A_CHUNK):
            rs = slice(n * A_CHUNK, (n + 1) * A_CHUNK)
            s = _dot(wm, vn[rs, :]) + bias
            ya_scr[rs, cs] = (u[rs, cs] * s).astype(BF16)

    d_q = _dot(hb, wqkv_ref[:, 0:d])
    g_a = jax.nn.sigmoid(d_ga)
    gb_ref[0] = jax.nn.sigmoid(d_gb).astype(BF16)
    d_a = _dot(ya_scr[...], wa_ref[...])
    d_k = _dot(hb, wqkv_ref[:, d:2 * d])
    apart_ref[0] = (g_a * d_a).astype(BF16)
    l2_heads(conv_stage(0, d_q), q_ref, hd ** -0.5)
    d_vv = _dot(hb, wqkv_ref[:, 2 * d:3 * d])
    l2_heads(conv_stage(1, d_k), k_ref, 1.0)
    d_z = _dot(hb, wz_ref[...])
    v_ref[0] = conv_stage(2, d_vv).astype(BF16)
    zs_ref[0] = _silu(d_z).astype(BF16)

    ab = _dot(hb, wab_ref[...])
    lane = lax.broadcasted_iota(jnp.int32, ab.shape, 1)
    is_beta = lane < HEADS
    g_col = -jnp.exp(alog16_ref[...]) * _softplus(ab + dtb16_ref[...])
    g_col = jnp.where(is_beta, 0.0, g_col)
    beta_col = jax.nn.sigmoid(ab)
    abt = _dot_nt(wabt_ref[...], hb)
    g_row = -jnp.exp(alogc_ref[...]) * _softplus(abt[HEADS:2 * HEADS, :] + dtbc_ref[...])
    gc = GDN_CHUNK
    ri = lax.broadcasted_iota(jnp.int32, (gc, gc), 0)
    ci = lax.broadcasted_iota(jnp.int32, (gc, gc), 1)
    lower = jnp.where(ci <= ri, 1.0, 0.0)
    upper = jnp.where(ri <= ci, 1.0, 0.0)
    is_beta_c = lax.broadcasted_iota(jnp.int32, (gc, 2 * HEADS), 1) < HEADS
    for n in range(tm // gc):
        rs = slice(n * gc, (n + 1) * gc)
        gam_col = jnp.dot(lower, g_col[rs, :], preferred_element_type=F32,
                          precision=lax.Precision.HIGHEST)
        gcol_ref[0, rs, :] = jnp.where(is_beta_c, beta_col[rs, :], gam_col)
        grow_ref[0, :, rs] = jnp.dot(g_row[:, rs], upper, preferred_element_type=F32,
                                     precision=lax.Precision.HIGHEST)


def _in_proj(x, mod, n1g, wuv, wqkv, wz, wab, wabt, wg, convw, sw, sbt, vg, alog16, dtb16,
             alogc, dtbc, wa):
    B, T, D = x.shape
    tm = TM_IN
    tok = lambda b, t: (b, t, 0)
    bf_out = jax.ShapeDtypeStruct((B, T, D), BF16)
    consts = [n1g, wuv, wqkv, wz, wab, wabt, wg, convw, sw, sbt, vg, alog16, dtb16, alogc,
              dtbc, wa]
    return pl.pallas_call(
        _in_proj_body,
        grid=(B, T // tm),
        in_specs=[pl.BlockSpec((1, tm, D), tok),
                  pl.BlockSpec((1, 6, D), lambda b, t: (b, 0, 0))]
        + [_const_spec(a.shape) for a in consts],
        out_specs=[pl.BlockSpec((1, tm, D), tok)] * 6
        + [pl.BlockSpec((1, tm, 2 * HEADS), tok),
           pl.BlockSpec((1, HEADS, tm), lambda b, t: (b, 0, t))],
        out_shape=[bf_out] * 6
        + [jax.ShapeDtypeStruct((B, T, 2 * HEADS), F32),
           jax.ShapeDtypeStruct((B, HEADS, T), F32)],
        scratch_shapes=[pltpu.VMEM((3, tm + CONV_PAD, D), F32),
                        pltpu.VMEM((CONV_PAD, 3 * D), F32),
                        pltpu.VMEM((tm, D), BF16)],
        compiler_params=pltpu.CompilerParams(
            dimension_semantics=("arbitrary", "arbitrary"),
            vmem_limit_bytes=V7X_VMEM_LIMIT),
        name="in_proj",
    )(x, mod, *consts)


def _gdn_body(q_ref, k_ref, v_ref, zs_ref, gcol_ref, grow_ref, og_ref, y_ref, s_scr):
    c = GDN_CHUNK
    nch = q_ref.shape[1] // c
    hd = HEAD_DIM
    t = pl.program_id(1)

    @pl.when(t == 0)
    def _():
        s_scr[...] = jnp.zeros(s_scr.shape, F32)

    row = lax.broadcasted_iota(jnp.int32, (c, c), 0)
    col = lax.broadcasted_iota(jnp.int32, (c, c), 1)
    incl = col <= row
    strict = col < row
    nblk = c // INV_BLOCK
    sh = INV_BLOCK.bit_length() - 1
    lane_c = lax.broadcasted_iota(jnp.int32, (INV_BLOCK, c), 1)
    row_c = lax.broadcasted_iota(jnp.int32, (INV_BLOCK, c), 0)
    lane_blk = lane_c >> sh
    lane_in = lane_c & (INV_BLOCK - 1)
    spread = jnp.where((row >> sh) == (col >> sh), 1.0, 0.0).astype(BF16)
    eye_c = jnp.where(lane_in == row_c, 1.0, 0.0)
    merge_masks = []
    s = INV_BLOCK
    while s < c:
        b = s.bit_length() - 1
        merge_masks.append((((row >> b) & 1) == 1) & ((col >> b) == (row >> b) - 1))
        s *= 2

    hs = range(HEADS)
    sl = [slice(h * hd, (h + 1) * hd) for h in hs]
    rows = [slice(n * c, (n + 1) * c) for n in range(nch)]
    ps = range(nch * HEADS)
    pn = [p // HEADS for p in ps]
    ph = [p % HEADS for p in ps]
    gcs = [gcol_ref[0, rows[n], :] for n in range(nch)]
    grs = [grow_ref[0, :, rows[n]] for n in range(nch)]
    qh = [q_ref[0, rows[pn[p]], sl[ph[p]]] for p in ps]
    kh = [k_ref[0, rows[pn[p]], sl[ph[p]]] for p in ps]
    beta = [gcs[pn[p]][:, ph[p]:ph[p] + 1] for p in ps]
    gam = [gcs[pn[p]][:, HEADS + ph[p]:HEADS + ph[p] + 1] for p in ps]
    g_last = [gcs[pn[p]][c - 1:c, HEADS + ph[p]:HEADS + ph[p] + 1] for p in ps]
    gam_r = [grs[pn[p]][ph[p]:ph[p] + 1, :] for p in ps]
    kb = [kh[p].astype(F32) * beta[p] for p in ps]
    kk = [_dot_nt(kb[p].astype(BF16), kh[p]) for p in ps]
    qk = [_dot_nt(qh[p], kh[p]) for p in ps]
    dec = [jnp.where(incl, jnp.exp(jnp.minimum(gam[p] - gam_r[p], 0.0)), 0.0) for p in ps]
    a_mat = [jnp.where(strict, kk[p] * dec[p], 0.0) for p in ps]
    qkd = [(qk[p] * dec[p]).astype(BF16) for p in ps]

    coef = []
    for p in ps:
        a_diag = jnp.zeros((INV_BLOCK, c), F32)
        for blk in range(nblk):
            a_diag = a_diag + jnp.where(
                lane_blk == blk, a_mat[p][blk * INV_BLOCK:(blk + 1) * INV_BLOCK, :], 0.0)
        picked = jnp.concatenate(
            [jnp.where(lane_in == j, a_diag, 0.0) for j in range(INV_BLOCK)], axis=0)
        coef.append(_dot(picked.astype(BF16), spread))
    x_inv = [eye_c for _ in ps]
    for j in range(INV_BLOCK - 1):
        x_inv = [x_inv[p] - coef[p][j * INV_BLOCK:(j + 1) * INV_BLOCK, :] * x_inv[p][j:j + 1, :]
                 for p in ps]
    t_mat = [jnp.concatenate([jnp.where(lane_blk == blk, x_inv[p], 0.0) for blk in range(nblk)],
                             axis=0) for p in ps]
    for mask in merge_masks:
        tb = [t_mat[p].astype(BF16) for p in ps]
        inner = [_dot(jnp.where(mask, a_mat[p], 0.0).astype(BF16), tb[p]).astype(BF16) for p in ps]
        t_mat = [t_mat[p] - _dot(tb[p], inner[p]) for p in ps]

    e_gam = [jnp.exp(gam[p]) for p in ps]
    rhs = [jnp.concatenate([v_ref[0, rows[pn[p]], sl[ph[p]]].astype(F32) * beta[p],
                            kb[p] * e_gam[p]], axis=1).astype(BF16) for p in ps]
    uw = [_dot(t_mat[p].astype(BF16), rhs[p]) for p in ps]
    wq = [jnp.concatenate([uw[p][:, hd:], qh[p].astype(F32) * e_gam[p]], axis=0).astype(BF16)
          for p in ps]
    k_dec = [(kh[p].astype(F32) * jnp.exp(g_last[p] - gam[p])).astype(BF16) for p in ps]

    s_cur = [s_scr[h] for h in hs]
    for n in range(nch):
        pp = [n * HEADS + h for h in hs]
        ws = [_dot(wq[pp[h]], s_cur[h].astype(BF16)) for h in hs]
        v_new = [(uw[pp[h]][:, :hd] - ws[h][:c]).astype(BF16) for h in hs]
        o = [ws[h][c:] + _dot(qkd[pp[h]], v_new[h]) for h in hs]
        s_cur = [s_cur[h] * jnp.exp(g_last[pp[h]]) + _dot_tn(k_dec[pp[h]], v_new[h]) for h in hs]
        for h in hs:
            on = o[h] * lax.rsqrt(jnp.mean(o[h] * o[h], axis=-1, keepdims=True) + NORM_EPS)
            y_ref[0, rows[n], sl[h]] = (
                on * og_ref[...] * zs_ref[0, rows[n], sl[h]].astype(F32)).astype(BF16)
    for h in hs:
        s_scr[h] = s_cur[h]


def _gdn(q, k, v, zs, gcol, grow, og):
    B, T, D = q.shape
    c = GDN_CHUNK * GDN_STEP_CHUNKS
    tok = lambda b, t: (b, t, 0)
    return pl.pallas_call(
        _gdn_body,
        grid=(B, T // c),
        in_specs=[pl.BlockSpec((1, c, D), tok)] * 4
        + [pl.BlockSpec((1, c, 2 * HEADS), tok),
           pl.BlockSpec((1, HEADS, c), lambda b, t: (b, 0, t)),
           _const_spec(og.shape)],
        out_specs=pl.BlockSpec((1, c, D), tok),
        out_shape=jax.ShapeDtypeStruct((B, T, D), BF16),
        scratch_shapes=[pltpu.VMEM((HEADS, HEAD_DIM, HEAD_DIM), F32)],
        compiler_params=pltpu.CompilerParams(
            dimension_semantics=("arbitrary", "arbitrary")),
        name="gdn",
    )(q, k, v, zs, gcol, grow, og)


def _out_ffn_body(x_ref, apart_ref, gb_ref, yb_ref, mod_ref, wb_ref, wo_ref, n2g_ref,
                  wfi_ref, wfo_ref, fg_ref, o_ref, *, final_norm, ffn_split):
    f = wfo_ref.shape[0]
    x = x_ref[0]
    merged = apart_ref[0].astype(F32) + gb_ref[0].astype(F32) * _dot(yb_ref[0], wb_ref[...])
    x = x + mod_ref[0, 2:3, :] * _dot(merged.astype(BF16), wo_ref[...])
    ms = jnp.mean(x * x, axis=-1, keepdims=True)
    h = x * lax.rsqrt(ms + NORM_EPS) * n2g_ref[...]
    h = h * (1.0 + mod_ref[0, 4:5, :]) + mod_ref[0, 3:4, :]
    hb = h.astype(BF16)
    fc = f // ffn_split

    def gate_up(j):
        return (_dot(hb, wfi_ref[:, j * fc:(j + 1) * fc]),
                _dot(hb, wfi_ref[:, f + j * fc:f + (j + 1) * fc]))

    nxt = gate_up(0)
    acc = None
    for j in range(ffn_split):
        gate, up = nxt
        if j + 1 < ffn_split:
            nxt = gate_up(j + 1)
        part = _dot((_silu(gate) * up).astype(BF16), wfo_ref[j * fc:(j + 1) * fc, :])
        acc = part if acc is None else acc + part
    x = x + mod_ref[0, 5:6, :] * acc
    if final_norm:
        ms = jnp.mean(x * x, axis=-1, keepdims=True)
        x = x * lax.rsqrt(ms + NORM_EPS) * fg_ref[...]
    o_ref[0] = x


def _out_ffn(x, apart, gb, yb, mod, wb, wo, n2g, wfi, wfo, fg, final_norm):
    B, T, D = x.shape
    tm = TM_OUT
    tok = lambda b, t: (b, t, 0)
    consts = [wb, wo, n2g, wfi, wfo, fg]
    return pl.pallas_call(
        functools.partial(_out_ffn_body, final_norm=final_norm, ffn_split=FFN_SPLIT),
        grid=(B, T // tm),
        in_specs=[pl.BlockSpec((1, tm, D), tok)] * 4
        + [pl.BlockSpec((1, 6, D), lambda b, t: (b, 0, 0))]
        + [_const_spec(a.shape) for a in consts],
        out_specs=pl.BlockSpec((1, tm, D), tok),
        out_shape=jax.ShapeDtypeStruct((B, T, D), F32),
        compiler_params=pltpu.CompilerParams(
            dimension_semantics=("arbitrary", "arbitrary"),
            vmem_limit_bytes=V7X_VMEM_LIMIT),
        name="out_ffn",
    )(x, apart, gb, yb, mod, *consts)


def kernel(x, c, ada_w, ada_b, norm1_g, w_in, conv_w, spatial_w, spatial_b, v_norm_g, a_log,
           dt_bias, o_norm_g, w_branch_a, w_branch_b, w_out, norm2_g, w_ffn_in, w_ffn_out,
           final_g):
    B, T, D = x.shape
    L = ada_w.shape[0]
    assert D == A_GROUPS * A_CHUNK == HEADS * HEAD_DIM
    assert T % TM_IN == 0 and T % TM_OUT == 0 and TM_IN % GDN_CHUNK == 0
    assert conv_w.shape[1] == CONV_K

    mod_all = _ada_mod(c, ada_w, ada_b).reshape(L, B, 6, D)
    o_uv, o_qkv, o_z, o_ab, o_g = 0, 2 * D, 5 * D, 6 * D, 6 * D + 2 * HEADS
    zeros8 = jnp.zeros((HEADS,), F32)
    for i in range(L):
        wi = w_in[i]
        wab = wi[:, o_ab:o_g].astype(BF16)
        apart, gb, q, k, v, zs, gcol, grow = _in_proj(
            x, mod_all[i], norm1_g[i].reshape(1, D),
            wi[:, o_uv:o_qkv].astype(BF16), wi[:, o_qkv:o_z].astype(BF16),
            wi[:, o_z:o_ab].astype(BF16), wab, wab.T, wi[:, o_g:].astype(BF16),
            conv_w[i], spatial_w[i], spatial_b[i].T, v_norm_g[i].reshape(1, D),
            jnp.concatenate([zeros8, a_log[i]]).reshape(1, 2 * HEADS),
            jnp.concatenate([zeros8, dt_bias[i]]).reshape(1, 2 * HEADS),
            a_log[i].reshape(HEADS, 1), dt_bias[i].reshape(HEADS, 1),
            w_branch_a[i].astype(BF16))
        yb = _gdn(q, k, v, zs, gcol, grow, o_norm_g[i].reshape(1, HEAD_DIM))
        x = _out_ffn(x, apart, gb, yb, mod_all[i], w_branch_b[i].astype(BF16),
                     w_out[i].astype(BF16), norm2_g[i].reshape(1, D),
                     w_ffn_in[i].astype(BF16), w_ffn_out[i].astype(BF16),
                     final_g.reshape(1, D), final_norm=(i == L - 1))
    return x
```

```python
import functools

import jax
import jax.numpy as jnp
from jax import lax
from jax.experimental import pallas as pl
from jax.experimental.pallas import tpu as pltpu

NORM_EPS = 1e-6
A_GROUPS = 8
A_CHUNK = 128
HEADS = 8
HEAD_DIM = 128
CONV_K = 4
GDN_CHUNK = 128
INV_BLOCK = 16
CONV_PAD = 8

V7X_VMEM_LIMIT = 56 * 1024 * 1024

TM_IN = 512
TM_OUT = 512
GDN_STEP_CHUNKS = 2
IN_SLAB = 256
FFN_SPLIT = 11

BF16 = jnp.bfloat16
F32 = jnp.float32
LOG2E = 1.4426950408889634


def _dot(a, b):
    return jnp.dot(a, b, preferred_element_type=F32)


def _dot_nt(a, b):
    return lax.dot_general(a, b, (((1,), (1,)), ((), ())), preferred_element_type=F32)


def _dot_tn(a, b):
    return lax.dot_general(a, b, (((0,), (0,)), ((), ())), preferred_element_type=F32)


def _softplus(x):
    return jnp.maximum(x, 0.0) + jnp.log1p(jnp.exp(-jnp.abs(x)))


def _gelu_tanh(x):
    a = -2.0 * 0.7978845608028654 * LOG2E
    return x / (1.0 + jnp.exp2(x * (a + (a * 0.044715) * (x * x))))


def _silu(x):
    return x * jax.nn.sigmoid(x)


def _const_spec(shape):
    nd = len(shape)
    return pl.BlockSpec(shape, lambda *_: (0,) * nd, pipeline_mode=pl.Buffered(1))


def _ada_body(c_ref, w_ref, b_ref, o_ref):
    cond = _silu(c_ref[...]).astype(BF16)
    o_ref[0] = _dot(cond, w_ref[0].astype(BF16)) + b_ref[0]


def _ada_mod(c, ada_w, ada_b):
    L, D, D6 = ada_w.shape
    B = c.shape[0]
    nblk = D6 // D
    return pl.pallas_call(
        _ada_body,
        grid=(L, nblk),
        in_specs=[
            pl.BlockSpec((B, D), lambda l, j: (0, 0)),
            pl.BlockSpec((1, D, D), lambda l, j: (l, 0, j)),
            pl.BlockSpec((1, 1, D), lambda l, j: (l, 0, j)),
        ],
        out_specs=pl.BlockSpec((1, B, D), lambda l, j: (l, 0, j)),
        out_shape=jax.ShapeDtypeStruct((L, B, D6), F32),
        name="ada_mod",
    )(c, ada_w, ada_b.reshape(L, 1, D6))


def _in_proj_body(x_ref, mod_ref, n1g_ref, wuv_ref, wqkv_ref, wz_ref, wab_ref, wabt_ref,
                  wg_ref, convw_ref, sw_ref, sbt_ref, vg_ref, alog16_ref, dtb16_ref,
                  alogc_ref, dtbc_ref, wa_ref,
                  apart_ref, gb_ref, q_ref, k_ref, v_ref, zs_ref, gcol_ref, grow_ref,
                  carry_scr, ya_scr, u_scr):
    tm, d = x_ref.shape[1], x_ref.shape[2]
    t = pl.program_id(1)

    @pl.when(t == 0)
    def _():
        carry_scr[...] = jnp.zeros(carry_scr.shape, F32)

    x = x_ref[0]
    ms = jnp.mean(x * x, axis=-1, keepdims=True)
    h = x * lax.rsqrt(ms + NORM_EPS) * n1g_ref[...]
    h = h * (1.0 + mod_ref[0, 1:2, :]) + mod_ref[0, 0:1, :]
    hb = h.astype(BF16)

    hd = HEAD_DIM
    gd = d // A_GROUPS
    sw_w = IN_SLAB
    row = lax.broadcasted_iota(jnp.int32, (A_CHUNK, A_CHUNK), 0)
    col = lax.broadcasted_iota(jnp.int32, (A_CHUNK, A_CHUNK), 1)
    causal = col <= row

    def ep_u(cs, pre):
        u_scr[:, cs] = _gelu_tanh(pre)

    def ep_v(cs, pre):
        v = _gelu_tanh(pre)
        for gl in range(sw_w // gd):
            g = cs.start // gd + gl
            gs = slice(g * gd, (g + 1) * gd)
            vg = v[:, gl * gd:(gl + 1) * gd]
            vn = vg * lax.rsqrt(jnp.mean(vg * vg, axis=-1, keepdims=True) + NORM_EPS)
            vn = (vn * vg_ref[:, gs]).astype(BF16)
            wm = jnp.where(causal, sw_ref[g], 0.0).astype(BF16)
            bias = sbt_ref[:, g:g + 1]
            for n in range(tm // A_CHUNK):
                rs = slice(n * A_CHUNK, (n + 1) * A_CHUNK)
                s = _dot(wm, vn[rs, :]) + bias
                ya_scr[rs, gs] = (u_scr[rs, gs] * s).astype(BF16)

    def ep_gb(cs, pre):
        gb_ref[0, :, cs] = jax.nn.sigmoid(pre).astype(BF16)

    def ep_apart(cs, pre):
        d_ga, d_a = pre
        apart_ref[0, :, cs] = (jax.nn.sigmoid(d_ga) * d_a).astype(BF16)

    row8 = lax.broadcasted_iota(jnp.int32, (CONV_PAD, sw_w), 0)

    def shift_rows(a, prev_tail, k):
        r = pltpu.roll(a, k, axis=0)
        head = jnp.where(row8 < k, pltpu.roll(prev_tail, k, axis=0), r[0:CONV_PAD, :])
        return jnp.concatenate([head, r[CONV_PAD:, :]], axis=0)

    def conv_silu(blk, cs, pre):
        bs = slice(blk * d + cs.start, blk * d + cs.stop)
        x1 = shift_rows(pre, carry_scr[0, :, bs], 1)
        z = pre * convw_ref[1:2, bs] + x1 * convw_ref[0:1, bs]
        acc = (pre * convw_ref[3:4, bs] + x1 * convw_ref[2:3, bs]
               + shift_rows(z, carry_scr[1, :, bs], 2))
        carry_scr[0, :, bs] = pre[tm - CONV_PAD:tm, :]
        carry_scr[1, :, bs] = z[tm - CONV_PAD:tm, :]
        return _silu(acc)

    def l2_heads(cs, act, out_ref, scale):
        for hl in range(sw_w // hd):
            ah = act[:, hl * hd:(hl + 1) * hd]
            inv = lax.rsqrt(jnp.sum(ah * ah, axis=-1, keepdims=True) + NORM_EPS) * scale
            out_ref[0, :, cs.start + hl * hd:cs.start + (hl + 1) * hd] = (ah * inv).astype(BF16)

    def ep_q(cs, pre):
        l2_heads(cs, conv_silu(0, cs, pre), q_ref, hd ** -0.5)

    def ep_k(cs, pre):
        l2_heads(cs, conv_silu(1, cs, pre), k_ref, 1.0)

    def ep_vv(cs, pre):
        v_ref[0, :, cs] = conv_silu(2, cs, pre).astype(BF16)

    def ep_z(cs, pre):
        zs_ref[0, :, cs] = _silu(pre).astype(BF16)

    slabs = [slice(c0, c0 + sw_w) for c0 in range(0, d, sw_w)]

    def proj(w_ref, off):
        return lambda cs: _dot(hb, w_ref[:, off + cs.start:off + cs.stop])

    def proj_apart(cs):
        return (_dot(hb, wg_ref[:, cs]), _dot(ya_scr[...], wa_ref[:, cs]))

    stages = [(proj(wuv_ref, 0), ep_u), (proj(wuv_ref, d), ep_v), (proj(wg_ref, d), ep_gb),
              (proj(wqkv_ref, 0), ep_q), (proj_apart, ep_apart), (proj(wqkv_ref, d), ep_k),
              (proj(wqkv_ref, 2 * d), ep_vv), (proj(wz_ref, 0), ep_z)]
    items = [(mm, ep, cs) for mm, ep in stages for cs in slabs]
    pending = None
    for mm, ep, cs in items:
        pre = mm(cs)
        if pending is not None:
            pending[0](pending[1], pending[2])
        pending = (ep, cs, pre)
    pending[0](pending[1], pending[2])

    ab = _dot(hb, wab_ref[...])
    lane = lax.broadcasted_iota(jnp.int32, ab.shape, 1)
    is_beta = lane < HEADS
    g_col = -jnp.exp(alog16_ref[...]) * _softplus(ab + dtb16_ref[...])
    g_col = jnp.where(is_beta, 0.0, g_col)
    beta_col = jax.nn.sigmoid(ab)
    abt = _dot_nt(wabt_ref[...], hb)
    g_row = -jnp.exp(alogc_ref[...]) * _softplus(abt[HEADS:2 * HEADS, :] + dtbc_ref[...])
    gc = GDN_CHUNK
    ri = lax.broadcasted_iota(jnp.int32, (gc, gc), 0)
    ci = lax.broadcasted_iota(jnp.int32, (gc, gc), 1)
    lower = jnp.where(ci <= ri, 1.0, 0.0)
    upper = jnp.where(ri <= ci, 1.0, 0.0)
    is_beta_c = lax.broadcasted_iota(jnp.int32, (gc, 2 * HEADS), 1) < HEADS
    for n in range(tm // gc):
        rs = slice(n * gc, (n + 1) * gc)
        gam_col = jnp.dot(lower, g_col[rs, :], preferred_element_type=F32,
                          precision=lax.Precision.HIGHEST)
        gcol_ref[0, rs, :] = jnp.where(is_beta_c, beta_col[rs, :], gam_col)
        grow_ref[0, :, rs] = jnp.dot(g_row[:, rs], upper, preferred_element_type=F32,
                                     precision=lax.Precision.HIGHEST)


def _in_proj(x, mod, n1g, wuv, wqkv, wz, wab, wabt, wg, convw, sw, sbt, vg, alog16, dtb16,
             alogc, dtbc, wa):
    B, T, D = x.shape
    tm = TM_IN
    tok = lambda b, t: (b, t, 0)
    bf_out = jax.ShapeDtypeStruct((B, T, D), BF16)
    consts = [n1g, wuv, wqkv, wz, wab, wabt, wg, convw, sw, sbt, vg, alog16, dtb16, alogc,
              dtbc, wa]
    return pl.pallas_call(
        _in_proj_body,
        grid=(B, T // tm),
        in_specs=[pl.BlockSpec((1, tm, D), tok),
                  pl.BlockSpec((1, 6, D), lambda b, t: (b, 0, 0))]
        + [_const_spec(a.shape) for a in consts],
        out_specs=[pl.BlockSpec((1, tm, D), tok)] * 6
        + [pl.BlockSpec((1, tm, 2 * HEADS), tok),
           pl.BlockSpec((1, HEADS, tm), lambda b, t: (b, 0, t))],
        out_shape=[bf_out] * 6
        + [jax.ShapeDtypeStruct((B, T, 2 * HEADS), F32),
           jax.ShapeDtypeStruct((B, HEADS, T), F32)],
        scratch_shapes=[pltpu.VMEM((2, CONV_PAD, 3 * D), F32),
                        pltpu.VMEM((tm, D), BF16),
                        pltpu.VMEM((tm, D), F32)],
        compiler_params=pltpu.CompilerParams(
            dimension_semantics=("arbitrary", "arbitrary"),
            vmem_limit_bytes=V7X_VMEM_LIMIT),
        name="in_proj",
    )(x, mod, *consts)


def _gdn_body(q_ref, k_ref, v_ref, zs_ref, gcol_ref, grow_ref, og_ref, y_ref, s_scr):
    c = GDN_CHUNK
    nch = q_ref.shape[1] // c
    hd = HEAD_DIM
    t = pl.program_id(1)

    @pl.when(t == 0)
    def _():
        s_scr[...] = jnp.zeros(s_scr.shape, F32)

    row = lax.broadcasted_iota(jnp.int32, (c, c), 0)
    col = lax.broadcasted_iota(jnp.int32, (c, c), 1)
    incl = col <= row
    strict = col < row
    nblk = c // INV_BLOCK
    sh = INV_BLOCK.bit_length() - 1
    lane_c = lax.broadcasted_iota(jnp.int32, (INV_BLOCK, c), 1)
    row_c = lax.broadcasted_iota(jnp.int32, (INV_BLOCK, c), 0)
    lane_blk = lane_c >> sh
    lane_in = lane_c & (INV_BLOCK - 1)
    spread = jnp.where((row >> sh) == (col >> sh), 1.0, 0.0).astype(BF16)
    eye_c = jnp.where(lane_in == row_c, 1.0, 0.0)
    merge_masks = []
    s = INV_BLOCK
    while s < c:
        b = s.bit_length() - 1
        merge_masks.append((((row >> b) & 1) == 1) & ((col >> b) == (row >> b) - 1))
        s *= 2

    hs = range(HEADS)
    sl = [slice(h * hd, (h + 1) * hd) for h in hs]
    rows = [slice(n * c, (n + 1) * c) for n in range(nch)]
    ps = range(nch * HEADS)
    pn = [p // HEADS for p in ps]
    ph = [p % HEADS for p in ps]
    gcs = [gcol_ref[0, rows[n], :] for n in range(nch)]
    grs = [grow_ref[0, :, rows[n]] for n in range(nch)]
    qh = [q_ref[0, rows[pn[p]], sl[ph[p]]] for p in ps]
    kh = [k_ref[0, rows[pn[p]], sl[ph[p]]] for p in ps]
    beta = [gcs[pn[p]][:, ph[p]:ph[p] + 1] for p in ps]
    gam = [gcs[pn[p]][:, HEADS + ph[p]:HEADS + ph[p] + 1] for p in ps]
    g_last = [gcs[pn[p]][c - 1:c, HEADS + ph[p]:HEADS + ph[p] + 1] for p in ps]
    gam_r = [grs[pn[p]][ph[p]:ph[p] + 1, :] for p in ps]
    kb = [kh[p].astype(F32) * beta[p] for p in ps]
    kk = [_dot_nt(kb[p].astype(BF16), kh[p]) for p in ps]
    qk = [_dot_nt(qh[p], kh[p]) for p in ps]
    dec = [jnp.where(incl, jnp.exp(jnp.minimum(gam[p] - gam_r[p], 0.0)), 0.0) for p in ps]
    a_mat = [jnp.where(strict, kk[p] * dec[p], 0.0) for p in ps]
    qkd = [(qk[p] * dec[p]).astype(BF16) for p in ps]

    coef = []
    for p in ps:
        a_diag = jnp.zeros((INV_BLOCK, c), F32)
        for blk in range(nblk):
            a_diag = a_diag + jnp.where(
                lane_blk == blk, a_mat[p][blk * INV_BLOCK:(blk + 1) * INV_BLOCK, :], 0.0)
        picked = jnp.concatenate(
            [jnp.where(lane_in == j, a_diag, 0.0) for j in range(INV_BLOCK)], axis=0)
        coef.append(_dot(picked.astype(BF16), spread))
    x_inv = [eye_c for _ in ps]
    for j in range(INV_BLOCK - 1):
        x_inv = [x_inv[p] - coef[p][j * INV_BLOCK:(j + 1) * INV_BLOCK, :] * x_inv[p][j:j + 1, :]
                 for p in ps]
    t_mat = [jnp.concatenate([jnp.where(lane_blk == blk, x_inv[p], 0.0) for blk in range(nblk)],
                             axis=0) for p in ps]
    for mask in merge_masks:
        tb = [t_mat[p].astype(BF16) for p in ps]
        inner = [_dot(jnp.where(mask, a_mat[p], 0.0).astype(BF16), tb[p]).astype(BF16) for p in ps]
        t_mat = [t_mat[p] - _dot(tb[p], inner[p]) for p in ps]

    e_gam = [jnp.exp(gam[p]) for p in ps]
    rhs = [jnp.concatenate([v_ref[0, rows[pn[p]], sl[ph[p]]].astype(F32) * beta[p],
                            kb[p] * e_gam[p]], axis=1).astype(BF16) for p in ps]
    uw = [_dot(t_mat[p].astype(BF16), rhs[p]) for p in ps]
    wq = [jnp.concatenate([uw[p][:, hd:], qh[p].astype(F32) * e_gam[p]], axis=0).astype(BF16)
          for p in ps]
    k_dec = [(kh[p].astype(F32) * jnp.exp(g_last[p] - gam[p])).astype(BF16) for p in ps]

    s_cur = [s_scr[h] for h in hs]
    for n in range(nch):
        pp = [n * HEADS + h for h in hs]
        ws = [_dot(wq[pp[h]], s_cur[h].astype(BF16)) for h in hs]
        v_new = [(uw[pp[h]][:, :hd] - ws[h][:c]).astype(BF16) for h in hs]
        o = [ws[h][c:] + _dot(qkd[pp[h]], v_new[h]) for h in hs]
        s_cur = [s_cur[h] * jnp.exp(g_last[pp[h]]) + _dot_tn(k_dec[pp[h]], v_new[h]) for h in hs]
        for h in hs:
            on = o[h] * lax.rsqrt(jnp.mean(o[h] * o[h], axis=-1, keepdims=True) + NORM_EPS)
            y_ref[0, rows[n], sl[h]] = (
                on * og_ref[...] * zs_ref[0, rows[n], sl[h]].astype(F32)).astype(BF16)
    for h in hs:
        s_scr[h] = s_cur[h]


def _gdn(q, k, v, zs, gcol, grow, og):
    B, T, D = q.shape
    c = GDN_CHUNK * GDN_STEP_CHUNKS
    tok = lambda b, t: (b, t, 0)
    return pl.pallas_call(
        _gdn_body,
        grid=(B, T // c),
        in_specs=[pl.BlockSpec((1, c, D), tok)] * 4
        + [pl.BlockSpec((1, c, 2 * HEADS), tok),
           pl.BlockSpec((1, HEADS, c), lambda b, t: (b, 0, t)),
           _const_spec(og.shape)],
        out_specs=pl.BlockSpec((1, c, D), tok),
        out_shape=jax.ShapeDtypeStruct((B, T, D), BF16),
        scratch_shapes=[pltpu.VMEM((HEADS, HEAD_DIM, HEAD_DIM), F32)],
        compiler_params=pltpu.CompilerParams(
            dimension_semantics=("arbitrary", "arbitrary")),
        name="gdn",
    )(q, k, v, zs, gcol, grow, og)


def _out_ffn_body(x_ref, apart_ref, gb_ref, yb_ref, mod_ref, wb_ref, wo_ref, n2g_ref,
                  wfi_ref, wfo_ref, fg_ref, o_ref, *, final_norm, ffn_split):
    f = wfo_ref.shape[0]
    x = x_ref[0]
    merged = apart_ref[0].astype(F32) + gb_ref[0].astype(F32) * _dot(yb_ref[0], wb_ref[...])
    x = x + mod_ref[0, 2:3, :] * _dot(merged.astype(BF16), wo_ref[...])
    ms = jnp.mean(x * x, axis=-1, keepdims=True)
    h = x * lax.rsqrt(ms + NORM_EPS) * n2g_ref[...]
    h = h * (1.0 + mod_ref[0, 4:5, :]) + mod_ref[0, 3:4, :]
    hb = h.astype(BF16)
    fc = f // ffn_split

    def gate_up(j):
        return (_dot(hb, wfi_ref[:, j * fc:(j + 1) * fc]),
                _dot(hb, wfi_ref[:, f + j * fc:f + (j + 1) * fc]))

    nxt = gate_up(0)
    acc = None
    for j in range(ffn_split):
        gate, up = nxt
        if j + 1 < ffn_split:
            nxt = gate_up(j + 1)
        part = _dot((_silu(gate) * up).astype(BF16), wfo_ref[j * fc:(j + 1) * fc, :])
        acc = part if acc is None else acc + part
    x = x + mod_ref[0, 5:6, :] * acc
    if final_norm:
        ms = jnp.mean(x * x, axis=-1, keepdims=True)
        x = x * lax.rsqrt(ms + NORM_EPS) * fg_ref[...]
    o_ref[0] = x


def _out_ffn(x, apart, gb, yb, mod, wb, wo, n2g, wfi, wfo, fg, final_norm):
    B, T, D = x.shape
    tm = TM_OUT
    tok = lambda b, t: (b, t, 0)
    consts = [wb, wo, n2g, wfi, wfo, fg]
    return pl.pallas_call(
        functools.partial(_out_ffn_body, final_norm=final_norm, ffn_split=FFN_SPLIT),
        grid=(B, T // tm),
        in_specs=[pl.BlockSpec((1, tm, D), tok)] * 4
        + [pl.BlockSpec((1, 6, D), lambda b, t: (b, 0, 0))]
        + [_const_spec(a.shape) for a in consts],
        out_specs=pl.BlockSpec((1, tm, D), tok),
        out_shape=jax.ShapeDtypeStruct((B, T, D), F32),
        compiler_params=pltpu.CompilerParams(
            dimension_semantics=("arbitrary", "arbitrary"),
            vmem_limit_bytes=V7X_VMEM_LIMIT),
        name="out_ffn",
    )(x, apart, gb, yb, mod, *consts)


def kernel(x, c, ada_w, ada_b, norm1_g, w_in, conv_w, spatial_w, spatial_b, v_norm_g, a_log,
           dt_bias, o_norm_g, w_branch_a, w_branch_b, w_out, norm2_g, w_ffn_in, w_ffn_out,
           final_g):
    B, T, D = x.shape
    L = ada_w.shape[0]
    assert D == A_GROUPS * A_CHUNK == HEADS * HEAD_DIM
    assert T % TM_IN == 0 and T % TM_OUT == 0 and TM_IN % GDN_CHUNK == 0
    assert conv_w.shape[1] == CONV_K

    mod_all = _ada_mod(c, ada_w, ada_b).reshape(L, B, 6, D)
    o_uv, o_qkv, o_z, o_ab, o_g = 0, 2 * D, 5 * D, 6 * D, 6 * D + 2 * HEADS
    zeros8 = jnp.zeros((HEADS,), F32)
    for i in range(L):
        wi = w_in[i]
        wab = wi[:, o_ab:o_g].astype(BF16)
        apart, gb, q, k, v, zs, gcol, grow = _in_proj(
            x, mod_all[i], norm1_g[i].reshape(1, D),
            wi[:, o_uv:o_qkv].astype(BF16), wi[:, o_qkv:o_z].astype(BF16),
            wi[:, o_z:o_ab].astype(BF16), wab, wab.T, wi[:, o_g:].astype(BF16),
            conv_w[i], spatial_w[i], spatial_b[i].T, v_norm_g[i].reshape(1, D),
            jnp.concatenate([zeros8, a_log[i]]).reshape(1, 2 * HEADS),
            jnp.concatenate([zeros8, dt_bias[i]]).reshape(1, 2 * HEADS),
            a_log[i].reshape(HEADS, 1), dt_bias[i].reshape(HEADS, 1),
            w_branch_a[i].astype(BF16))
        yb = _gdn(q, k, v, zs, gcol, grow, o_norm_g[i].reshape(1, HEAD_DIM))
        x = _out_ffn(x, apart, gb, yb, mod_all[i], w_branch_b[i].astype(BF16),
                     w_out[i].astype(BF16), norm2_g[i].reshape(1, D),
                     w_ffn_in[i].astype(BF16), w_ffn_out[i].astype(BF16),
                     final_g.reshape(1, D), final_norm=(i == L - 1))
    return x
```

```python
import functools
import math

import jax
import jax.numpy as jnp
from jax import lax
from jax.experimental import pallas as pl
from jax.experimental.pallas import tpu as pltpu

NORM_EPS = 1e-6
A_GROUPS = 8
A_CHUNK = 128
HEADS = 8
HEAD_DIM = 128
CONV_K = 4
GDN_CHUNK = 128
INV_BLOCK = 16
BF16_SUBLANES = 16
CONV_PAD = 8

V7X_VMEM_LIMIT = 56 * 1024 * 1024

TM_IN = 512
TM_OUT = 512
GDN_STEP_CHUNKS = 2
IN_SLAB = 256
FFN_SPLIT = 11

BF16 = jnp.bfloat16
F32 = jnp.float32
LOG2E = 1.4426950408889634


def _dot(a, b):
    return jnp.dot(a, b, preferred_element_type=F32)


def _dot_nt(a, b):
    return lax.dot_general(a, b, (((1,), (1,)), ((), ())), preferred_element_type=F32)


def _dot_tn(a, b):
    return lax.dot_general(a, b, (((0,), (0,)), ((), ())), preferred_element_type=F32)


def _softplus(x):
    return jnp.maximum(x, 0.0) + jnp.log1p(jnp.exp(-jnp.abs(x)))


def _gelu_tanh(x):
    a = -2.0 * 0.7978845608028654 * LOG2E
    return x / (1.0 + jnp.exp2(x * (a + (a * 0.044715) * (x * x))))


def _silu(x):
    return x * jax.nn.sigmoid(x)


def _const_spec(shape):
    nd = len(shape)
    return pl.BlockSpec(shape, lambda *_: (0,) * nd, pipeline_mode=pl.Buffered(1))


def _cast_blocks(rows, nsteps):
    nblk = math.gcd(rows // BF16_SUBLANES, nsteps)
    return rows // nblk, nsteps // nblk


def _cast_spec(shape, nsteps, nt, layer=None):
    rb, rep = _cast_blocks(shape[0], nsteps)
    if layer is None:
        return pl.BlockSpec((rb, shape[1]), lambda b, t: ((b * nt + t) // rep, 0))
    return pl.BlockSpec((None, rb, shape[1]), lambda b, t: (layer, (b * nt + t) // rep, 0))


def _ada_body(c_ref, w_ref, b_ref, o_ref):
    cond = _silu(c_ref[...]).astype(BF16)
    o_ref[0] = _dot(cond, w_ref[0].astype(BF16)) + b_ref[0]


def _ada_mod(c, ada_w, ada_b):
    L, D, D6 = ada_w.shape
    B = c.shape[0]
    nblk = D6 // D
    return pl.pallas_call(
        _ada_body,
        grid=(L, nblk),
        in_specs=[
            pl.BlockSpec((B, D), lambda l, j: (0, 0)),
            pl.BlockSpec((1, D, D), lambda l, j: (l, 0, j)),
            pl.BlockSpec((1, 1, D), lambda l, j: (l, 0, j)),
        ],
        out_specs=pl.BlockSpec((1, B, D), lambda l, j: (l, 0, j)),
        out_shape=jax.ShapeDtypeStruct((L, B, D6), F32),
        name="ada_mod",
    )(c, ada_w, ada_b.reshape(L, 1, D6))


def _in_proj_body(x_ref, mod_ref, n1g_ref, wuv_ref, wqkv_ref, wz_ref, wab_ref, wabt_ref,
                  wg_ref, convw_ref, sw_ref, sbt_ref, vg_ref, alog16_ref, dtb16_ref,
                  alogc_ref, dtbc_ref, wa_ref, wb_f, wo_f, wfi_f, wfo_f,
                  apart_ref, gb_ref, q_ref, k_ref, v_ref, zs_ref, gcol_ref, grow_ref,
                  wb_o, wo_o, wfi_o, wfo_o,
                  carry_scr, ya_scr, u_scr):
    tm, d = x_ref.shape[1], x_ref.shape[2]
    t = pl.program_id(1)

    @pl.when(t == 0)
    def _():
        carry_scr[...] = jnp.zeros(carry_scr.shape, F32)

    for src, dst in ((wb_f, wb_o), (wo_f, wo_o), (wfi_f, wfi_o), (wfo_f, wfo_o)):
        dst[...] = src[...].astype(BF16)

    x = x_ref[0]
    ms = jnp.mean(x * x, axis=-1, keepdims=True)
    h = x * lax.rsqrt(ms + NORM_EPS) * n1g_ref[...]
    h = h * (1.0 + mod_ref[0, 1:2, :]) + mod_ref[0, 0:1, :]
    hb = h.astype(BF16)

    hd = HEAD_DIM
    gd = d // A_GROUPS
    sw_w = IN_SLAB
    row = lax.broadcasted_iota(jnp.int32, (A_CHUNK, A_CHUNK), 0)
    col = lax.broadcasted_iota(jnp.int32, (A_CHUNK, A_CHUNK), 1)
    causal = col <= row

    def ep_u(cs, pre):
        u_scr[:, cs] = _gelu_tanh(pre)

    def ep_v(cs, pre):
        v = _gelu_tanh(pre)
        for gl in range(sw_w // gd):
            g = cs.start // gd + gl
            gs = slice(g * gd, (g + 1) * gd)
            vg = v[:, gl * gd:(gl + 1) * gd]
            vn = vg * lax.rsqrt(jnp.mean(vg * vg, axis=-1, keepdims=True) + NORM_EPS)
            vn = (vn * vg_ref[:, gs]).astype(BF16)
            wm = jnp.where(causal, sw_ref[g], 0.0).astype(BF16)
            bias = sbt_ref[:, g:g + 1]
            for n in range(tm // A_CHUNK):
                rs = slice(n * A_CHUNK, (n + 1) * A_CHUNK)
                s = _dot(wm, vn[rs, :]) + bias
                ya_scr[rs, gs] = (u_scr[rs, gs] * s).astype(BF16)

    def ep_gb(cs, pre):
        gb_ref[0, :, cs] = jax.nn.sigmoid(pre).astype(BF16)

    def ep_apart(cs, pre):
        d_ga, d_a = pre
        apart_ref[0, :, cs] = (jax.nn.sigmoid(d_ga) * d_a).astype(BF16)

    row8 = lax.broadcasted_iota(jnp.int32, (CONV_PAD, sw_w), 0)

    def shift_rows(a, prev_tail, k):
        r = pltpu.roll(a, k, axis=0)
        head = jnp.where(row8 < k, pltpu.roll(prev_tail, k, axis=0), r[0:CONV_PAD, :])
        return jnp.concatenate([head, r[CONV_PAD:, :]], axis=0)

    def conv_silu(blk, cs, pre):
        bs = slice(blk * d + cs.start, blk * d + cs.stop)
        x1 = shift_rows(pre, carry_scr[0, :, bs], 1)
        z = pre * convw_ref[1:2, bs] + x1 * convw_ref[0:1, bs]
        acc = (pre * convw_ref[3:4, bs] + x1 * convw_ref[2:3, bs]
               + shift_rows(z, carry_scr[1, :, bs], 2))
        carry_scr[0, :, bs] = pre[tm - CONV_PAD:tm, :]
        carry_scr[1, :, bs] = z[tm - CONV_PAD:tm, :]
        return _silu(acc)

    def l2_heads(cs, act, out_ref, scale):
        for hl in range(sw_w // hd):
            ah = act[:, hl * hd:(hl + 1) * hd]
            inv = lax.rsqrt(jnp.sum(ah * ah, axis=-1, keepdims=True) + NORM_EPS) * scale
            out_ref[0, :, cs.start + hl * hd:cs.start + (hl + 1) * hd] = (ah * inv).astype(BF16)

    def ep_q(cs, pre):
        l2_heads(cs, conv_silu(0, cs, pre), q_ref, hd ** -0.5)

    def ep_k(cs, pre):
        l2_heads(cs, conv_silu(1, cs, pre), k_ref, 1.0)

    def ep_vv(cs, pre):
        v_ref[0, :, cs] = conv_silu(2, cs, pre).astype(BF16)

    def ep_z(cs, pre):
        zs_ref[0, :, cs] = _silu(pre).astype(BF16)

    slabs = [slice(c0, c0 + sw_w) for c0 in range(0, d, sw_w)]

    def proj(w_ref, off):
        return lambda cs: _dot(hb, w_ref[:, off + cs.start:off + cs.stop])

    def proj_apart(cs):
        return (_dot(hb, wg_ref[:, cs]), _dot(ya_scr[...], wa_ref[:, cs]))

    stages = [(proj(wuv_ref, 0), ep_u), (proj(wuv_ref, d), ep_v), (proj(wg_ref, d), ep_gb),
              (proj(wqkv_ref, 0), ep_q), (proj_apart, ep_apart), (proj(wqkv_ref, d), ep_k),
              (proj(wqkv_ref, 2 * d), ep_vv), (proj(wz_ref, 0), ep_z)]
    items = [(mm, ep, cs) for mm, ep in stages for cs in slabs]
    pending = None
    for mm, ep, cs in items:
        pre = mm(cs)
        if pending is not None:
            pending[0](pending[1], pending[2])
        pending = (ep, cs, pre)
    pending[0](pending[1], pending[2])

    ab = _dot(hb, wab_ref[...])
    lane = lax.broadcasted_iota(jnp.int32, ab.shape, 1)
    is_beta = lane < HEADS
    g_col = -jnp.exp(alog16_ref[...]) * _softplus(ab + dtb16_ref[...])
    g_col = jnp.where(is_beta, 0.0, g_col)
    beta_col = jax.nn.sigmoid(ab)
    abt = _dot_nt(wabt_ref[...], hb)
    g_row = -jnp.exp(alogc_ref[...]) * _softplus(abt[HEADS:2 * HEADS, :] + dtbc_ref[...])
    gc = GDN_CHUNK
    ri = lax.broadcasted_iota(jnp.int32, (gc, gc), 0)
    ci = lax.broadcasted_iota(jnp.int32, (gc, gc), 1)
    lower = jnp.where(ci <= ri, 1.0, 0.0)
    upper = jnp.where(ri <= ci, 1.0, 0.0)
    is_beta_c = lax.broadcasted_iota(jnp.int32, (gc, 2 * HEADS), 1) < HEADS
    for n in range(tm // gc):
        rs = slice(n * gc, (n + 1) * gc)
        gam_col = jnp.dot(lower, g_col[rs, :], preferred_element_type=F32,
                          precision=lax.Precision.HIGHEST)
        gcol_ref[0, rs, :] = jnp.where(is_beta_c, beta_col[rs, :], gam_col)
        grow_ref[0, :, rs] = jnp.dot(g_row[:, rs], upper, preferred_element_type=F32,
                                     precision=lax.Precision.HIGHEST)


def _in_proj(x, mod, n1g, wuv, wqkv, wz, wab, wabt, wg, convw, sw, sbt, vg, alog16, dtb16,
             alogc, dtbc, wa, cast_f32, layer):
    B, T, D = x.shape
    tm = TM_IN
    nsteps = B * (T // tm)
    tok = lambda b, t: (b, t, 0)
    bf_out = jax.ShapeDtypeStruct((B, T, D), BF16)
    consts = [n1g, wuv, wqkv, wz, wab, wabt, wg, convw, sw, sbt, vg, alog16, dtb16, alogc,
              dtbc, wa]
    return pl.pallas_call(
        _in_proj_body,
        grid=(B, T // tm),
        in_specs=[pl.BlockSpec((1, tm, D), tok),
                  pl.BlockSpec((1, 6, D), lambda b, t: (b, 0, 0))]
        + [_const_spec(a.shape) for a in consts]
        + [_cast_spec(a.shape[1:], nsteps, T // tm, layer) for a in cast_f32],
        out_specs=[pl.BlockSpec((1, tm, D), tok)] * 6
        + [pl.BlockSpec((1, tm, 2 * HEADS), tok),
           pl.BlockSpec((1, HEADS, tm), lambda b, t: (b, 0, t))]
        + [_cast_spec(a.shape[1:], nsteps, T // tm) for a in cast_f32],
        out_shape=[bf_out] * 6
        + [jax.ShapeDtypeStruct((B, T, 2 * HEADS), F32),
           jax.ShapeDtypeStruct((B, HEADS, T), F32)]
        + [jax.ShapeDtypeStruct(a.shape[1:], BF16) for a in cast_f32],
        scratch_shapes=[pltpu.VMEM((2, CONV_PAD, 3 * D), F32),
                        pltpu.VMEM((tm, D), BF16),
                        pltpu.VMEM((tm, D), F32)],
        compiler_params=pltpu.CompilerParams(
            dimension_semantics=("arbitrary", "arbitrary"),
            vmem_limit_bytes=V7X_VMEM_LIMIT),
        name="in_proj",
    )(x, mod, *consts, *cast_f32)


def _gdn_body(q_ref, k_ref, v_ref, zs_ref, gcol_ref, grow_ref, og_ref, y_ref, s_scr):
    c = GDN_CHUNK
    nch = q_ref.shape[1] // c
    hd = HEAD_DIM
    t = pl.program_id(1)

    @pl.when(t == 0)
    def _():
        s_scr[...] = jnp.zeros(s_scr.shape, F32)

    row = lax.broadcasted_iota(jnp.int32, (c, c), 0)
    col = lax.broadcasted_iota(jnp.int32, (c, c), 1)
    incl = col <= row
    strict = col < row
    nblk = c // INV_BLOCK
    sh = INV_BLOCK.bit_length() - 1
    lane_c = lax.broadcasted_iota(jnp.int32, (INV_BLOCK, c), 1)
    row_c = lax.broadcasted_iota(jnp.int32, (INV_BLOCK, c), 0)
    lane_blk = lane_c >> sh
    lane_in = lane_c & (INV_BLOCK - 1)
    spread = jnp.where((row >> sh) == (col >> sh), 1.0, 0.0).astype(BF16)
    eye_c = jnp.where(lane_in == row_c, 1.0, 0.0)
    merge_masks = []
    s = INV_BLOCK
    while s < c:
        b = s.bit_length() - 1
        merge_masks.append((((row >> b) & 1) == 1) & ((col >> b) == (row >> b) - 1))
        s *= 2

    hs = range(HEADS)
    sl = [slice(h * hd, (h + 1) * hd) for h in hs]
    rows = [slice(n * c, (n + 1) * c) for n in range(nch)]
    ps = range(nch * HEADS)
    pn = [p // HEADS for p in ps]
    ph = [p % HEADS for p in ps]
    gcs = [gcol_ref[0, rows[n], :] for n in range(nch)]
    grs = [grow_ref[0, :, rows[n]] for n in range(nch)]
    qh = [q_ref[0, rows[pn[p]], sl[ph[p]]] for p in ps]
    kh = [k_ref[0, rows[pn[p]], sl[ph[p]]] for p in ps]
    beta = [gcs[pn[p]][:, ph[p]:ph[p] + 1] for p in ps]
    gam = [gcs[pn[p]][:, HEADS + ph[p]:HEADS + ph[p] + 1] for p in ps]
    g_last = [gcs[pn[p]][c - 1:c, HEADS + ph[p]:HEADS + ph[p] + 1] for p in ps]
    gam_r = [grs[pn[p]][ph[p]:ph[p] + 1, :] for p in ps]
    kb = [kh[p].astype(F32) * beta[p] for p in ps]
    kk = [_dot_nt(kb[p].astype(BF16), kh[p]) for p in ps]
    qk = [_dot_nt(qh[p], kh[p]) for p in ps]
    dec = [jnp.where(incl, jnp.exp(jnp.minimum(gam[p] - gam_r[p], 0.0)), 0.0) for p in ps]
    a_mat = [jnp.where(strict, kk[p] * dec[p], 0.0) for p in ps]
    qkd = [(qk[p] * dec[p]).astype(BF16) for p in ps]

    coef = []
    for p in ps:
        a_diag = jnp.zeros((INV_BLOCK, c), F32)
        for blk in range(nblk):
            a_diag = a_diag + jnp.where(
                lane_blk == blk, a_mat[p][blk * INV_BLOCK:(blk + 1) * INV_BLOCK, :], 0.0)
        picked = jnp.concatenate(
            [jnp.where(lane_in == j, a_diag, 0.0) for j in range(INV_BLOCK)], axis=0)
        coef.append(_dot(picked.astype(BF16), spread))
    x_inv = [eye_c for _ in ps]
    for j in range(INV_BLOCK - 1):
        x_inv = [x_inv[p] - coef[p][j * INV_BLOCK:(j + 1) * INV_BLOCK, :] * x_inv[p][j:j + 1, :]
                 for p in ps]
    t_mat = [jnp.concatenate([jnp.where(lane_blk == blk, x_inv[p], 0.0) for blk in range(nblk)],
                             axis=0) for p in ps]
    for mask in merge_masks:
        tb = [t_mat[p].astype(BF16) for p in ps]
        inner = [_dot(jnp.where(mask, a_mat[p], 0.0).astype(BF16), tb[p]).astype(BF16) for p in ps]
        t_mat = [t_mat[p] - _dot(tb[p], inner[p]) for p in ps]

    e_gam = [jnp.exp(gam[p]) for p in ps]
    rhs = [jnp.concatenate([v_ref[0, rows[pn[p]], sl[ph[p]]].astype(F32) * beta[p],
                            kb[p] * e_gam[p]], axis=1).astype(BF16) for p in ps]
    uw = [_dot(t_mat[p].astype(BF16), rhs[p]) for p in ps]
    wq = [jnp.concatenate([uw[p][:, hd:], qh[p].astype(F32) * e_gam[p]], axis=0).astype(BF16)
          for p in ps]
    k_dec = [(kh[p].astype(F32) * jnp.exp(g_last[p] - gam[p])).astype(BF16) for p in ps]

    s_cur = [s_scr[h] for h in hs]
    for n in range(nch):
        pp = [n * HEADS + h for h in hs]
        ws = [_dot(wq[pp[h]], s_cur[h].astype(BF16)) for h in hs]
        v_new = [(uw[pp[h]][:, :hd] - ws[h][:c]).astype(BF16) for h in hs]
        o = [ws[h][c:] + _dot(qkd[pp[h]], v_new[h]) for h in hs]
        s_cur = [s_cur[h] * jnp.exp(g_last[pp[h]]) + _dot_tn(k_dec[pp[h]], v_new[h]) for h in hs]
        for h in hs:
            on = o[h] * lax.rsqrt(jnp.mean(o[h] * o[h], axis=-1, keepdims=True) + NORM_EPS)
            y_ref[0, rows[n], sl[h]] = (
                on * og_ref[...] * zs_ref[0, rows[n], sl[h]].astype(F32)).astype(BF16)
    for h in hs:
        s_scr[h] = s_cur[h]


def _gdn(q, k, v, zs, gcol, grow, og):
    B, T, D = q.shape
    c = GDN_CHUNK * GDN_STEP_CHUNKS
    tok = lambda b, t: (b, t, 0)
    return pl.pallas_call(
        _gdn_body,
        grid=(B, T // c),
        in_specs=[pl.BlockSpec((1, c, D), tok)] * 4
        + [pl.BlockSpec((1, c, 2 * HEADS), tok),
           pl.BlockSpec((1, HEADS, c), lambda b, t: (b, 0, t)),
           _const_spec(og.shape)],
        out_specs=pl.BlockSpec((1, c, D), tok),
        out_shape=jax.ShapeDtypeStruct((B, T, D), BF16),
        scratch_shapes=[pltpu.VMEM((HEADS, HEAD_DIM, HEAD_DIM), F32)],
        compiler_params=pltpu.CompilerParams(
            dimension_semantics=("arbitrary", "arbitrary")),
        name="gdn",
    )(q, k, v, zs, gcol, grow, og)


def _out_ffn_body(x_ref, apart_ref, gb_ref, yb_ref, mod_ref, wb_ref, wo_ref, n2g_ref,
                  wfi_ref, wfo_ref, fg_ref, *rest, final_norm, ffn_split, cast_cols):
    f = wfo_ref.shape[0]
    if cast_cols:
        win_f, wa_f, o_ref, *piece_refs, wa_o = rest
        w_rows = win_f[...]
        for (c0, c1), dst in zip(cast_cols, piece_refs):
            dst[...] = w_rows[:, c0:c1].astype(BF16)
        wa_o[...] = wa_f[...].astype(BF16)
    else:
        (o_ref,) = rest
    x = x_ref[0]
    merged = apart_ref[0].astype(F32) + gb_ref[0].astype(F32) * _dot(yb_ref[0], wb_ref[...])
    x = x + mod_ref[0, 2:3, :] * _dot(merged.astype(BF16), wo_ref[...])
    ms = jnp.mean(x * x, axis=-1, keepdims=True)
    h = x * lax.rsqrt(ms + NORM_EPS) * n2g_ref[...]
    h = h * (1.0 + mod_ref[0, 4:5, :]) + mod_ref[0, 3:4, :]
    hb = h.astype(BF16)
    fc = f // ffn_split

    def gate_up(j):
        return (_dot(hb, wfi_ref[:, j * fc:(j + 1) * fc]),
                _dot(hb, wfi_ref[:, f + j * fc:f + (j + 1) * fc]))

    nxt = gate_up(0)
    acc = None
    for j in range(ffn_split):
        gate, up = nxt
        if j + 1 < ffn_split:
            nxt = gate_up(j + 1)
        part = _dot((_silu(gate) * up).astype(BF16), wfo_ref[j * fc:(j + 1) * fc, :])
        acc = part if acc is None else acc + part
    x = x + mod_ref[0, 5:6, :] * acc
    if final_norm:
        ms = jnp.mean(x * x, axis=-1, keepdims=True)
        x = x * lax.rsqrt(ms + NORM_EPS) * fg_ref[...]
    o_ref[0] = x


def _out_ffn(x, apart, gb, yb, mod, wb, wo, n2g, wfi, wfo, fg, final_norm, next_f32, layer,
             cast_cols):
    B, T, D = x.shape
    tm = TM_OUT
    nsteps = B * (T // tm)
    tok = lambda b, t: (b, t, 0)
    consts = [wb, wo, n2g, wfi, wfo, fg]
    cast_specs = [_cast_spec(a.shape[1:], nsteps, T // tm, layer) for a in next_f32]
    cast_out_shapes = []
    if next_f32:
        rows = next_f32[0].shape[1]
        cast_out_shapes = [(rows, c1 - c0) for c0, c1 in cast_cols] + [next_f32[1].shape[1:]]
    return pl.pallas_call(
        functools.partial(_out_ffn_body, final_norm=final_norm, ffn_split=FFN_SPLIT,
                          cast_cols=cast_cols),
        grid=(B, T // tm),
        in_specs=[pl.BlockSpec((1, tm, D), tok)] * 4
        + [pl.BlockSpec((1, 6, D), lambda b, t: (b, 0, 0))]
        + [_const_spec(a.shape) for a in consts] + cast_specs,
        out_specs=[pl.BlockSpec((1, tm, D), tok)]
        + [_cast_spec(sh, nsteps, T // tm) for sh in cast_out_shapes],
        out_shape=[jax.ShapeDtypeStruct((B, T, D), F32)]
        + [jax.ShapeDtypeStruct(sh, BF16) for sh in cast_out_shapes],
        compiler_params=pltpu.CompilerParams(
            dimension_semantics=("arbitrary", "arbitrary"),
            vmem_limit_bytes=V7X_VMEM_LIMIT),
        name="out_ffn",
    )(x, apart, gb, yb, mod, *consts, *next_f32)


def kernel(x, c, ada_w, ada_b, norm1_g, w_in, conv_w, spatial_w, spatial_b, v_norm_g, a_log,
           dt_bias, o_norm_g, w_branch_a, w_branch_b, w_out, norm2_g, w_ffn_in, w_ffn_out,
           final_g):
    B, T, D = x.shape
    L = ada_w.shape[0]
    assert D == A_GROUPS * A_CHUNK == HEADS * HEAD_DIM
    assert T % TM_IN == 0 and T % TM_OUT == 0 and TM_IN % GDN_CHUNK == 0
    assert conv_w.shape[1] == CONV_K

    mod_all = _ada_mod(c, ada_w, ada_b).reshape(L, B, 6, D)
    o_uv, o_qkv, o_z, o_ab, o_g = 0, 2 * D, 5 * D, 6 * D, 6 * D + 2 * HEADS
    piece_cols = ((o_uv, o_qkv), (o_qkv, o_z), (o_z, o_ab), (o_g, w_in.shape[2]))
    zeros8 = jnp.zeros((HEADS,), F32)
    in_w = [w_in[0][:, c0:c1].astype(BF16) for c0, c1 in piece_cols] + [w_branch_a[0].astype(BF16)]
    for i in range(L):
        wuv, wqkv, wz, wg, wa = in_w
        wab = w_in[i][:, o_ab:o_g].astype(BF16)
        apart, gb, q, k, v, zs, gcol, grow, wb, wo, wfi, wfo = _in_proj(
            x, mod_all[i], norm1_g[i].reshape(1, D), wuv, wqkv, wz, wab, wab.T, wg,
            conv_w[i], spatial_w[i], spatial_b[i].T, v_norm_g[i].reshape(1, D),
            jnp.concatenate([zeros8, a_log[i]]).reshape(1, 2 * HEADS),
            jnp.concatenate([zeros8, dt_bias[i]]).reshape(1, 2 * HEADS),
            a_log[i].reshape(HEADS, 1), dt_bias[i].reshape(HEADS, 1), wa,
            [w_branch_b, w_out, w_ffn_in, w_ffn_out], i)
        yb = _gdn(q, k, v, zs, gcol, grow, o_norm_g[i].reshape(1, HEAD_DIM))
        last = i == L - 1
        x, *in_w = _out_ffn(x, apart, gb, yb, mod_all[i], wb, wo, norm2_g[i].reshape(1, D),
                            wfi, wfo, final_g.reshape(1, D), final_norm=last,
                            next_f32=[] if last else [w_in, w_branch_a], layer=i + 1,
                            cast_cols=() if last else piece_cols)
    return x
```

```python
import functools
import math

import jax
import jax.numpy as jnp
from jax import lax
from jax.experimental import pallas as pl
from jax.experimental.pallas import tpu as pltpu

NORM_EPS = 1e-6
A_GROUPS = 8
A_CHUNK = 128
HEADS = 8
HEAD_DIM = 128
CONV_K = 4
GDN_CHUNK = 128
INV_BLOCK = 16
BF16_SUBLANES = 16
CONV_PAD = 8

V7X_VMEM_LIMIT = 56 * 1024 * 1024

TM_IN = 512
TM_OUT = 512
GDN_STEP_CHUNKS = 4
IN_SLAB = 256
FFN_SPLIT = 11

BF16 = jnp.bfloat16
F32 = jnp.float32
LOG2E = 1.4426950408889634


def _dot(a, b):
    return jnp.dot(a, b, preferred_element_type=F32)


def _dot_nt(a, b):
    return lax.dot_general(a, b, (((1,), (1,)), ((), ())), preferred_element_type=F32)


def _dot_tn(a, b):
    return lax.dot_general(a, b, (((0,), (0,)), ((), ())), preferred_element_type=F32)


def _softplus(x):
    return jnp.maximum(x, 0.0) + jnp.log1p(jnp.exp(-jnp.abs(x)))


def _gelu_tanh(x):
    a = -2.0 * 0.7978845608028654 * LOG2E
    return x / (1.0 + jnp.exp2(x * (a + (a * 0.044715) * (x * x))))


def _silu(x):
    return x * jax.nn.sigmoid(x)


def _const_spec(shape):
    nd = len(shape)
    return pl.BlockSpec(shape, lambda *_: (0,) * nd, pipeline_mode=pl.Buffered(1))


def _cast_blocks(rows, nsteps):
    nblk = math.gcd(rows // BF16_SUBLANES, nsteps)
    return rows // nblk, nsteps // nblk


def _cast_spec(shape, nsteps, nt, layer=None):
    rb, rep = _cast_blocks(shape[0], nsteps)
    if layer is None:
        return pl.BlockSpec((rb, shape[1]), lambda b, t: ((b * nt + t) // rep, 0))
    return pl.BlockSpec((None, rb, shape[1]), lambda b, t: (layer, (b * nt + t) // rep, 0))


def _ada_body(c_ref, w_ref, b_ref, o_ref):
    cond = _silu(c_ref[...]).astype(BF16)
    o_ref[0] = _dot(cond, w_ref[0].astype(BF16)) + b_ref[0]


def _ada_mod(c, ada_w, ada_b):
    L, D, D6 = ada_w.shape
    B = c.shape[0]
    nblk = D6 // D
    return pl.pallas_call(
        _ada_body,
        grid=(L, nblk),
        in_specs=[
            pl.BlockSpec((B, D), lambda l, j: (0, 0)),
            pl.BlockSpec((1, D, D), lambda l, j: (l, 0, j)),
            pl.BlockSpec((1, 1, D), lambda l, j: (l, 0, j)),
        ],
        out_specs=pl.BlockSpec((1, B, D), lambda l, j: (l, 0, j)),
        out_shape=jax.ShapeDtypeStruct((L, B, D6), F32),
        name="ada_mod",
    )(c, ada_w, ada_b.reshape(L, 1, D6))


def _in_proj_body(x_ref, mod_ref, n1g_ref, wuv_ref, wqkv_ref, wz_ref, wab_ref, wabt_ref,
                  wg_ref, convw_ref, sw_ref, sbt_ref, vg_ref, alog16_ref, dtb16_ref,
                  alogc_ref, dtbc_ref, wa_ref, wb_f, wo_f, wfi_f, wfo_f,
                  apart_ref, gb_ref, q_ref, k_ref, v_ref, zs_ref, gcol_ref, grow_ref,
                  wb_o, wo_o, wfi_o, wfo_o,
                  carry_scr, ya_scr, u_scr):
    tm, d = x_ref.shape[1], x_ref.shape[2]
    t = pl.program_id(1)

    @pl.when(t == 0)
    def _():
        carry_scr[...] = jnp.zeros(carry_scr.shape, F32)

    for src, dst in ((wb_f, wb_o), (wo_f, wo_o), (wfi_f, wfi_o), (wfo_f, wfo_o)):
        dst[...] = src[...].astype(BF16)

    x = x_ref[0]
    ms = jnp.mean(x * x, axis=-1, keepdims=True)
    h = x * lax.rsqrt(ms + NORM_EPS) * n1g_ref[...]
    h = h * (1.0 + mod_ref[0, 1:2, :]) + mod_ref[0, 0:1, :]
    hb = h.astype(BF16)

    hd = HEAD_DIM
    gd = d // A_GROUPS
    sw_w = IN_SLAB
    row = lax.broadcasted_iota(jnp.int32, (A_CHUNK, A_CHUNK), 0)
    col = lax.broadcasted_iota(jnp.int32, (A_CHUNK, A_CHUNK), 1)
    causal = col <= row

    def ep_u(cs, pre):
        u_scr[:, cs] = _gelu_tanh(pre)

    def ep_v(cs, pre):
        v = _gelu_tanh(pre)
        for gl in range(sw_w // gd):
            g = cs.start // gd + gl
            gs = slice(g * gd, (g + 1) * gd)
            vg = v[:, gl * gd:(gl + 1) * gd]
            vn = vg * lax.rsqrt(jnp.mean(vg * vg, axis=-1, keepdims=True) + NORM_EPS)
            vn = (vn * vg_ref[:, gs]).astype(BF16)
            wm = jnp.where(causal, sw_ref[g], 0.0).astype(BF16)
            bias = sbt_ref[:, g:g + 1]
            for n in range(tm // A_CHUNK):
                rs = slice(n * A_CHUNK, (n + 1) * A_CHUNK)
                s = _dot(wm, vn[rs, :]) + bias
                ya_scr[rs, gs] = (u_scr[rs, gs] * s).astype(BF16)

    def ep_gb(cs, pre):
        gb_ref[0, :, cs] = jax.nn.sigmoid(pre).astype(BF16)

    def ep_apart(cs, pre):
        d_ga, d_a = pre
        apart_ref[0, :, cs] = (jax.nn.sigmoid(d_ga) * d_a).astype(BF16)

    row8 = lax.broadcasted_iota(jnp.int32, (CONV_PAD, sw_w), 0)

    def shift_rows(a, prev_tail, k):
        r = pltpu.roll(a, k, axis=0)
        head = jnp.where(row8 < k, pltpu.roll(prev_tail, k, axis=0), r[0:CONV_PAD, :])
        return jnp.concatenate([head, r[CONV_PAD:, :]], axis=0)

    def conv_silu(blk, cs, pre):
        bs = slice(blk * d + cs.start, blk * d + cs.stop)
        x1 = shift_rows(pre, carry_scr[0, :, bs], 1)
        z = pre * convw_ref[1:2, bs] + x1 * convw_ref[0:1, bs]
        acc = (pre * convw_ref[3:4, bs] + x1 * convw_ref[2:3, bs]
               + shift_rows(z, carry_scr[1, :, bs], 2))
        carry_scr[0, :, bs] = pre[tm - CONV_PAD:tm, :]
        carry_scr[1, :, bs] = z[tm - CONV_PAD:tm, :]
        return _silu(acc)

    def l2_heads(cs, act, out_ref, scale):
        for hl in range(sw_w // hd):
            ah = act[:, hl * hd:(hl + 1) * hd]
            inv = lax.rsqrt(jnp.sum(ah * ah, axis=-1, keepdims=True) + NORM_EPS) * scale
            out_ref[0, :, cs.start + hl * hd:cs.start + (hl + 1) * hd] = (ah * inv).astype(BF16)

    def ep_q(cs, pre):
        l2_heads(cs, conv_silu(0, cs, pre), q_ref, hd ** -0.5)

    def ep_k(cs, pre):
        l2_heads(cs, conv_silu(1, cs, pre), k_ref, 1.0)

    def ep_vv(cs, pre):
        v_ref[0, :, cs] = conv_silu(2, cs, pre).astype(BF16)

    def ep_z(cs, pre):
        zs_ref[0, :, cs] = _silu(pre).astype(BF16)

    slabs = [slice(c0, c0 + sw_w) for c0 in range(0, d, sw_w)]

    def proj(w_ref, off):
        return lambda cs: _dot(hb, w_ref[:, off + cs.start:off + cs.stop])

    def proj_apart(cs):
        return (_dot(hb, wg_ref[:, cs]), _dot(ya_scr[...], wa_ref[:, cs]))

    stages = [(proj(wuv_ref, 0), ep_u), (proj(wuv_ref, d), ep_v), (proj(wg_ref, d), ep_gb),
              (proj(wqkv_ref, 0), ep_q), (proj_apart, ep_apart), (proj(wqkv_ref, d), ep_k),
              (proj(wqkv_ref, 2 * d), ep_vv), (proj(wz_ref, 0), ep_z)]
    items = [(mm, ep, cs) for mm, ep in stages for cs in slabs]
    pending = None
    for mm, ep, cs in items:
        pre = mm(cs)
        if pending is not None:
            pending[0](pending[1], pending[2])
        pending = (ep, cs, pre)
    pending[0](pending[1], pending[2])

    ab = _dot(hb, wab_ref[...])
    lane = lax.broadcasted_iota(jnp.int32, ab.shape, 1)
    is_beta = lane < HEADS
    g_col = -jnp.exp(alog16_ref[...]) * _softplus(ab + dtb16_ref[...])
    g_col = jnp.where(is_beta, 0.0, g_col)
    beta_col = jax.nn.sigmoid(ab)
    abt = _dot_nt(wabt_ref[...], hb)
    g_row = -jnp.exp(alogc_ref[...]) * _softplus(abt[HEADS:2 * HEADS, :] + dtbc_ref[...])
    gc = GDN_CHUNK
    ri = lax.broadcasted_iota(jnp.int32, (gc, gc), 0)
    ci = lax.broadcasted_iota(jnp.int32, (gc, gc), 1)
    lower = jnp.where(ci <= ri, 1.0, 0.0)
    upper = jnp.where(ri <= ci, 1.0, 0.0)
    is_beta_c = lax.broadcasted_iota(jnp.int32, (gc, 2 * HEADS), 1) < HEADS
    for n in range(tm // gc):
        rs = slice(n * gc, (n + 1) * gc)
        gam_col = jnp.dot(lower, g_col[rs, :], preferred_element_type=F32,
                          precision=lax.Precision.HIGHEST)
        gcol_ref[0, rs, :] = jnp.where(is_beta_c, beta_col[rs, :], gam_col)
        grow_ref[0, :, rs] = jnp.dot(g_row[:, rs], upper, preferred_element_type=F32,
                                     precision=lax.Precision.HIGHEST)


def _in_proj(x, mod, n1g, wuv, wqkv, wz, wab, wabt, wg, convw, sw, sbt, vg, alog16, dtb16,
             alogc, dtbc, wa, cast_f32, layer):
    B, T, D = x.shape
    tm = TM_IN
    nsteps = B * (T // tm)
    tok = lambda b, t: (b, t, 0)
    bf_out = jax.ShapeDtypeStruct((B, T, D), BF16)
    consts = [n1g, wuv, wqkv, wz, wab, wabt, wg, convw, sw, sbt, vg, alog16, dtb16, alogc,
              dtbc, wa]
    return pl.pallas_call(
        _in_proj_body,
        grid=(B, T // tm),
        in_specs=[pl.BlockSpec((1, tm, D), tok),
                  pl.BlockSpec((1, 6, D), lambda b, t: (b, 0, 0))]
        + [_const_spec(a.shape) for a in consts]
        + [_cast_spec(a.shape[1:], nsteps, T // tm, layer) for a in cast_f32],
        out_specs=[pl.BlockSpec((1, tm, D), tok)] * 6
        + [pl.BlockSpec((1, tm, 2 * HEADS), tok),
           pl.BlockSpec((1, HEADS, tm), lambda b, t: (b, 0, t))]
        + [_cast_spec(a.shape[1:], nsteps, T // tm) for a in cast_f32],
        out_shape=[bf_out] * 6
        + [jax.ShapeDtypeStruct((B, T, 2 * HEADS), F32),
           jax.ShapeDtypeStruct((B, HEADS, T), F32)]
        + [jax.ShapeDtypeStruct(a.shape[1:], BF16) for a in cast_f32],
        scratch_shapes=[pltpu.VMEM((2, CONV_PAD, 3 * D), F32),
                        pltpu.VMEM((tm, D), BF16),
                        pltpu.VMEM((tm, D), F32)],
        compiler_params=pltpu.CompilerParams(
            dimension_semantics=("arbitrary", "arbitrary"),
            vmem_limit_bytes=V7X_VMEM_LIMIT),
        name="in_proj",
    )(x, mod, *consts, *cast_f32)


def _gdn_body(q_ref, k_ref, v_ref, zs_ref, gcol_ref, grow_ref, og_ref, y_ref, s_scr):
    c = GDN_CHUNK
    nch = q_ref.shape[1] // c
    hd = HEAD_DIM
    t = pl.program_id(1)

    @pl.when(t == 0)
    def _():
        s_scr[...] = jnp.zeros(s_scr.shape, F32)

    row = lax.broadcasted_iota(jnp.int32, (c, c), 0)
    col = lax.broadcasted_iota(jnp.int32, (c, c), 1)
    incl = col <= row
    strict = col < row
    nblk = c // INV_BLOCK
    sh = INV_BLOCK.bit_length() - 1
    lane_c = lax.broadcasted_iota(jnp.int32, (INV_BLOCK, c), 1)
    row_c = lax.broadcasted_iota(jnp.int32, (INV_BLOCK, c), 0)
    lane_blk = lane_c >> sh
    lane_in = lane_c & (INV_BLOCK - 1)
    spread = jnp.where((row >> sh) == (col >> sh), 1.0, 0.0).astype(BF16)
    eye_c = jnp.where(lane_in == row_c, 1.0, 0.0)
    merge_masks = []
    s = INV_BLOCK
    while s < c:
        b = s.bit_length() - 1
        merge_masks.append((((row >> b) & 1) == 1) & ((col >> b) == (row >> b) - 1))
        s *= 2

    hs = range(HEADS)
    sl = [slice(h * hd, (h + 1) * hd) for h in hs]
    rows = [slice(n * c, (n + 1) * c) for n in range(nch)]
    ps = range(nch * HEADS)
    pn = [p // HEADS for p in ps]
    ph = [p % HEADS for p in ps]
    gcs = [gcol_ref[0, rows[n], :] for n in range(nch)]
    grs = [grow_ref[0, :, rows[n]] for n in range(nch)]
    qh = [q_ref[0, rows[pn[p]], sl[ph[p]]] for p in ps]
    kh = [k_ref[0, rows[pn[p]], sl[ph[p]]] for p in ps]
    beta = [gcs[pn[p]][:, ph[p]:ph[p] + 1] for p in ps]
    gam = [gcs[pn[p]][:, HEADS + ph[p]:HEADS + ph[p] + 1] for p in ps]
    g_last = [gcs[pn[p]][c - 1:c, HEADS + ph[p]:HEADS + ph[p] + 1] for p in ps]
    gam_r = [grs[pn[p]][ph[p]:ph[p] + 1, :] for p in ps]
    kb = [kh[p].astype(F32) * beta[p] for p in ps]
    kk = [_dot_nt(kb[p].astype(BF16), kh[p]) for p in ps]
    qk = [_dot_nt(qh[p], kh[p]) for p in ps]
    dec = [jnp.where(incl, jnp.exp(jnp.minimum(gam[p] - gam_r[p], 0.0)), 0.0) for p in ps]
    a_mat = [jnp.where(strict, kk[p] * dec[p], 0.0) for p in ps]
    qkd = [(qk[p] * dec[p]).astype(BF16) for p in ps]

    coef = []
    for p in ps:
        a_diag = jnp.zeros((INV_BLOCK, c), F32)
        for blk in range(nblk):
            a_diag = a_diag + jnp.where(
                lane_blk == blk, a_mat[p][blk * INV_BLOCK:(blk + 1) * INV_BLOCK, :], 0.0)
        picked = jnp.concatenate(
            [jnp.where(lane_in == j, a_diag, 0.0) for j in range(INV_BLOCK)], axis=0)
        coef.append(_dot(picked.astype(BF16), spread))
    x_inv = [eye_c for _ in ps]
    for j in range(INV_BLOCK - 1):
        x_inv = [x_inv[p] - coef[p][j * INV_BLOCK:(j + 1) * INV_BLOCK, :] * x_inv[p][j:j + 1, :]
                 for p in ps]
    t_mat = [jnp.concatenate([jnp.where(lane_blk == blk, x_inv[p], 0.0) for blk in range(nblk)],
                             axis=0) for p in ps]
    for mask in merge_masks:
        tb = [t_mat[p].astype(BF16) for p in ps]
        inner = [_dot(jnp.where(mask, a_mat[p], 0.0).astype(BF16), tb[p]).astype(BF16) for p in ps]
        t_mat = [t_mat[p] - _dot(tb[p], inner[p]) for p in ps]

    e_gam = [jnp.exp(gam[p]) for p in ps]
    rhs = [jnp.concatenate([v_ref[0, rows[pn[p]], sl[ph[p]]].astype(F32) * beta[p],
                            kb[p] * e_gam[p]], axis=1).astype(BF16) for p in ps]
    uw = [_dot(t_mat[p].astype(BF16), rhs[p]) for p in ps]
    wq = [jnp.concatenate([uw[p][:, hd:], qh[p].astype(F32) * e_gam[p]], axis=0).astype(BF16)
          for p in ps]
    k_dec = [(kh[p].astype(F32) * jnp.exp(g_last[p] - gam[p])).astype(BF16) for p in ps]

    s_cur = [s_scr[h] for h in hs]
    for n in range(nch):
        pp = [n * HEADS + h for h in hs]
        ws = [_dot(wq[pp[h]], s_cur[h].astype(BF16)) for h in hs]
        v_new = [(uw[pp[h]][:, :hd] - ws[h][:c]).astype(BF16) for h in hs]
        o = [ws[h][c:] + _dot(qkd[pp[h]], v_new[h]) for h in hs]
        s_cur = [s_cur[h] * jnp.exp(g_last[pp[h]]) + _dot_tn(k_dec[pp[h]], v_new[h]) for h in hs]
        for h in hs:
            on = o[h] * lax.rsqrt(jnp.mean(o[h] * o[h], axis=-1, keepdims=True) + NORM_EPS)
            y_ref[0, rows[n], sl[h]] = (
                on * og_ref[...] * zs_ref[0, rows[n], sl[h]].astype(F32)).astype(BF16)
    for h in hs:
        s_scr[h] = s_cur[h]


def _gdn(q, k, v, zs, gcol, grow, og):
    B, T, D = q.shape
    c = GDN_CHUNK * GDN_STEP_CHUNKS
    tok = lambda b, t: (b, t, 0)
    return pl.pallas_call(
        _gdn_body,
        grid=(B, T // c),
        in_specs=[pl.BlockSpec((1, c, D), tok)] * 4
        + [pl.BlockSpec((1, c, 2 * HEADS), tok),
           pl.BlockSpec((1, HEADS, c), lambda b, t: (b, 0, t)),
           _const_spec(og.shape)],
        out_specs=pl.BlockSpec((1, c, D), tok),
        out_shape=jax.ShapeDtypeStruct((B, T, D), BF16),
        scratch_shapes=[pltpu.VMEM((HEADS, HEAD_DIM, HEAD_DIM), F32)],
        compiler_params=pltpu.CompilerParams(
            dimension_semantics=("arbitrary", "arbitrary")),
        name="gdn",
    )(q, k, v, zs, gcol, grow, og)


def _out_ffn_body(x_ref, apart_ref, gb_ref, yb_ref, mod_ref, wb_ref, wo_ref, n2g_ref,
                  wfi_ref, wfo_ref, fg_ref, o_ref, *, final_norm, ffn_split):
    f = wfo_ref.shape[0]
    x = x_ref[0]
    merged = apart_ref[0].astype(F32) + gb_ref[0].astype(F32) * _dot(yb_ref[0], wb_ref[...])
    x = x + mod_ref[0, 2:3, :] * _dot(merged.astype(BF16), wo_ref[...])
    ms = jnp.mean(x * x, axis=-1, keepdims=True)
    h = x * lax.rsqrt(ms + NORM_EPS) * n2g_ref[...]
    h = h * (1.0 + mod_ref[0, 4:5, :]) + mod_ref[0, 3:4, :]
    hb = h.astype(BF16)
    fc = f // ffn_split

    def gate_up(j):
        return (_dot(hb, wfi_ref[:, j * fc:(j + 1) * fc]),
                _dot(hb, wfi_ref[:, f + j * fc:f + (j + 1) * fc]))

    nxt = gate_up(0)
    acc = None
    for j in range(ffn_split):
        gate, up = nxt
        if j + 1 < ffn_split:
            nxt = gate_up(j + 1)
        part = _dot((_silu(gate) * up).astype(BF16), wfo_ref[j * fc:(j + 1) * fc, :])
        acc = part if acc is None else acc + part
    x = x + mod_ref[0, 5:6, :] * acc
    if final_norm:
        ms = jnp.mean(x * x, axis=-1, keepdims=True)
        x = x * lax.rsqrt(ms + NORM_EPS) * fg_ref[...]
    o_ref[0] = x


def _out_ffn(x, apart, gb, yb, mod, wb, wo, n2g, wfi, wfo, fg, final_norm):
    B, T, D = x.shape
    tm = TM_OUT
    tok = lambda b, t: (b, t, 0)
    consts = [wb, wo, n2g, wfi, wfo, fg]
    return pl.pallas_call(
        functools.partial(_out_ffn_body, final_norm=final_norm, ffn_split=FFN_SPLIT),
        grid=(B, T // tm),
        in_specs=[pl.BlockSpec((1, tm, D), tok)] * 4
        + [pl.BlockSpec((1, 6, D), lambda b, t: (b, 0, 0))]
        + [_const_spec(a.shape) for a in consts],
        out_specs=pl.BlockSpec((1, tm, D), tok),
        out_shape=jax.ShapeDtypeStruct((B, T, D), F32),
        compiler_params=pltpu.CompilerParams(
            dimension_semantics=("arbitrary", "arbitrary"),
            vmem_limit_bytes=V7X_VMEM_LIMIT),
        name="out_ffn",
    )(x, apart, gb, yb, mod, *consts)


def kernel(x, c, ada_w, ada_b, norm1_g, w_in, conv_w, spatial_w, spatial_b, v_norm_g, a_log,
           dt_bias, o_norm_g, w_branch_a, w_branch_b, w_out, norm2_g, w_ffn_in, w_ffn_out,
           final_g):
    B, T, D = x.shape
    L = ada_w.shape[0]
    assert D == A_GROUPS * A_CHUNK == HEADS * HEAD_DIM
    assert T % TM_IN == 0 and T % TM_OUT == 0 and TM_IN % GDN_CHUNK == 0
    assert conv_w.shape[1] == CONV_K

    mod_all = _ada_mod(c, ada_w, ada_b).reshape(L, B, 6, D)
    o_uv, o_qkv, o_z, o_ab, o_g = 0, 2 * D, 5 * D, 6 * D, 6 * D + 2 * HEADS
    zeros8 = jnp.zeros((HEADS,), F32)
    for i in range(L):
        wi = w_in[i]
        wab = wi[:, o_ab:o_g].astype(BF16)
        apart, gb, q, k, v, zs, gcol, grow, wb, wo, wfi, wfo = _in_proj(
            x, mod_all[i], norm1_g[i].reshape(1, D),
            wi[:, o_uv:o_qkv].astype(BF16), wi[:, o_qkv:o_z].astype(BF16),
            wi[:, o_z:o_ab].astype(BF16), wab, wab.T, wi[:, o_g:].astype(BF16),
            conv_w[i], spatial_w[i], spatial_b[i].T, v_norm_g[i].reshape(1, D),
            jnp.concatenate([zeros8, a_log[i]]).reshape(1, 2 * HEADS),
            jnp.concatenate([zeros8, dt_bias[i]]).reshape(1, 2 * HEADS),
            a_log[i].reshape(HEADS, 1), dt_bias[i].reshape(HEADS, 1),
            w_branch_a[i].astype(BF16), [w_branch_b, w_out, w_ffn_in, w_ffn_out], i)
        yb = _gdn(q, k, v, zs, gcol, grow, o_norm_g[i].reshape(1, HEAD_DIM))
        x = _out_ffn(x, apart, gb, yb, mod_all[i], wb, wo, norm2_g[i].reshape(1, D), wfi, wfo,
                     final_g.reshape(1, D), final_norm=(i == L - 1))
    return x
```

```python
import functools
import math

import jax
import jax.numpy as jnp
from jax import lax
from jax.experimental import pallas as pl
from jax.experimental.pallas import tpu as pltpu

NORM_EPS = 1e-6
A_GROUPS = 8
A_CHUNK = 128
HEADS = 8
HEAD_DIM = 128
CONV_K = 4
GDN_CHUNK = 128
INV_BLOCK = 16
BF16_SUBLANES = 16
CONV_PAD = 8

V7X_VMEM_LIMIT = 56 * 1024 * 1024

TM_IN = 512
TM_OUT = 512
GDN_STEP_CHUNKS = 2
GDN_STEP_BATCH = 2
IN_SLAB = 256
FFN_SPLIT = 11

BF16 = jnp.bfloat16
F32 = jnp.float32
LOG2E = 1.4426950408889634


def _dot(a, b):
    return jnp.dot(a, b, preferred_element_type=F32)


def _dot_nt(a, b):
    return lax.dot_general(a, b, (((1,), (1,)), ((), ())), preferred_element_type=F32)


def _dot_tn(a, b):
    return lax.dot_general(a, b, (((0,), (0,)), ((), ())), preferred_element_type=F32)


def _softplus(x):
    return jnp.maximum(x, 0.0) + jnp.log1p(jnp.exp(-jnp.abs(x)))


def _gelu_tanh(x):
    a = -2.0 * 0.7978845608028654 * LOG2E
    return x / (1.0 + jnp.exp2(x * (a + (a * 0.044715) * (x * x))))


def _silu(x):
    return x * jax.nn.sigmoid(x)


def _const_spec(shape):
    nd = len(shape)
    return pl.BlockSpec(shape, lambda *_: (0,) * nd, pipeline_mode=pl.Buffered(1))


def _cast_blocks(rows, nsteps):
    nblk = math.gcd(rows // BF16_SUBLANES, nsteps)
    return rows // nblk, nsteps // nblk


def _cast_spec(shape, nsteps, nt, layer=None):
    rb, rep = _cast_blocks(shape[0], nsteps)
    if layer is None:
        return pl.BlockSpec((rb, shape[1]), lambda b, t: ((b * nt + t) // rep, 0))
    return pl.BlockSpec((None, rb, shape[1]), lambda b, t: (layer, (b * nt + t) // rep, 0))


def _ada_body(c_ref, w_ref, b_ref, o_ref):
    cond = _silu(c_ref[...]).astype(BF16)
    o_ref[0] = _dot(cond, w_ref[0].astype(BF16)) + b_ref[0]


def _ada_mod(c, ada_w, ada_b):
    L, D, D6 = ada_w.shape
    B = c.shape[0]
    nblk = D6 // D
    return pl.pallas_call(
        _ada_body,
        grid=(L, nblk),
        in_specs=[
            pl.BlockSpec((B, D), lambda l, j: (0, 0)),
            pl.BlockSpec((1, D, D), lambda l, j: (l, 0, j)),
            pl.BlockSpec((1, 1, D), lambda l, j: (l, 0, j)),
        ],
        out_specs=pl.BlockSpec((1, B, D), lambda l, j: (l, 0, j)),
        out_shape=jax.ShapeDtypeStruct((L, B, D6), F32),
        name="ada_mod",
    )(c, ada_w, ada_b.reshape(L, 1, D6))


def _in_proj_body(x_ref, mod_ref, n1g_ref, wuv_ref, wqkv_ref, wz_ref, wab_ref, wabt_ref,
                  wg_ref, convw_ref, sw_ref, sbt_ref, vg_ref, alog16_ref, dtb16_ref,
                  alogc_ref, dtbc_ref, wa_ref, wb_f, wo_f, wfi_f, wfo_f,
                  apart_ref, gb_ref, q_ref, k_ref, v_ref, zs_ref, gcol_ref, grow_ref,
                  wb_o, wo_o, wfi_o, wfo_o,
                  carry_scr, ya_scr, u_scr):
    tm, d = x_ref.shape[1], x_ref.shape[2]
    t = pl.program_id(1)

    @pl.when(t == 0)
    def _():
        carry_scr[...] = jnp.zeros(carry_scr.shape, F32)

    for src, dst in ((wb_f, wb_o), (wo_f, wo_o), (wfi_f, wfi_o), (wfo_f, wfo_o)):
        dst[...] = src[...].astype(BF16)

    x = x_ref[0]
    ms = jnp.mean(x * x, axis=-1, keepdims=True)
    h = x * lax.rsqrt(ms + NORM_EPS) * n1g_ref[...]
    h = h * (1.0 + mod_ref[0, 1:2, :]) + mod_ref[0, 0:1, :]
    hb = h.astype(BF16)

    hd = HEAD_DIM
    gd = d // A_GROUPS
    sw_w = IN_SLAB
    row = lax.broadcasted_iota(jnp.int32, (A_CHUNK, A_CHUNK), 0)
    col = lax.broadcasted_iota(jnp.int32, (A_CHUNK, A_CHUNK), 1)
    causal = col <= row

    def ep_u(cs, pre):
        u_scr[:, cs] = _gelu_tanh(pre)

    def ep_v(cs, pre):
        v = _gelu_tanh(pre)
        for gl in range(sw_w // gd):
            g = cs.start // gd + gl
            gs = slice(g * gd, (g + 1) * gd)
            vg = v[:, gl * gd:(gl + 1) * gd]
            vn = vg * lax.rsqrt(jnp.mean(vg * vg, axis=-1, keepdims=True) + NORM_EPS)
            vn = (vn * vg_ref[:, gs]).astype(BF16)
            wm = jnp.where(causal, sw_ref[g], 0.0).astype(BF16)
            bias = sbt_ref[:, g:g + 1]
            for n in range(tm // A_CHUNK):
                rs = slice(n * A_CHUNK, (n + 1) * A_CHUNK)
                s = _dot(wm, vn[rs, :]) + bias
                ya_scr[rs, gs] = (u_scr[rs, gs] * s).astype(BF16)

    def ep_gb(cs, pre):
        gb_ref[0, :, cs] = jax.nn.sigmoid(pre).astype(BF16)

    def ep_apart(cs, pre):
        d_ga, d_a = pre
        apart_ref[0, :, cs] = (jax.nn.sigmoid(d_ga) * d_a).astype(BF16)

    row8 = lax.broadcasted_iota(jnp.int32, (CONV_PAD, sw_w), 0)

    def shift_rows(a, prev_tail, k):
        r = pltpu.roll(a, k, axis=0)
        head = jnp.where(row8 < k, pltpu.roll(prev_tail, k, axis=0), r[0:CONV_PAD, :])
        return jnp.concatenate([head, r[CONV_PAD:, :]], axis=0)

    def conv_silu(blk, cs, pre):
        bs = slice(blk * d + cs.start, blk * d + cs.stop)
        x1 = shift_rows(pre, carry_scr[0, :, bs], 1)
        z = pre * convw_ref[1:2, bs] + x1 * convw_ref[0:1, bs]
        acc = (pre * convw_ref[3:4, bs] + x1 * convw_ref[2:3, bs]
               + shift_rows(z, carry_scr[1, :, bs], 2))
        carry_scr[0, :, bs] = pre[tm - CONV_PAD:tm, :]
        carry_scr[1, :, bs] = z[tm - CONV_PAD:tm, :]
        return _silu(acc)

    def l2_heads(cs, act, out_ref, scale):
        for hl in range(sw_w // hd):
            ah = act[:, hl * hd:(hl + 1) * hd]
            inv = lax.rsqrt(jnp.sum(ah * ah, axis=-1, keepdims=True) + NORM_EPS) * scale
            out_ref[0, :, cs.start + hl * hd:cs.start + (hl + 1) * hd] = (ah * inv).astype(BF16)

    def ep_q(cs, pre):
        l2_heads(cs, conv_silu(0, cs, pre), q_ref, hd ** -0.5)

    def ep_k(cs, pre):
        l2_heads(cs, conv_silu(1, cs, pre), k_ref, 1.0)

    def ep_vv(cs, pre):
        v_ref[0, :, cs] = conv_silu(2, cs, pre).astype(BF16)

    def ep_z(cs, pre):
        zs_ref[0, :, cs] = _silu(pre).astype(BF16)

    slabs = [slice(c0, c0 + sw_w) for c0 in range(0, d, sw_w)]

    def proj(w_ref, off):
        return lambda cs: _dot(hb, w_ref[:, off + cs.start:off + cs.stop])

    def proj_apart(cs):
        return (_dot(hb, wg_ref[:, cs]), _dot(ya_scr[...], wa_ref[:, cs]))

    stages = [(proj(wuv_ref, 0), ep_u), (proj(wuv_ref, d), ep_v), (proj(wg_ref, d), ep_gb),
              (proj(wqkv_ref, 0), ep_q), (proj_apart, ep_apart), (proj(wqkv_ref, d), ep_k),
              (proj(wqkv_ref, 2 * d), ep_vv), (proj(wz_ref, 0), ep_z)]
    items = [(mm, ep, cs) for mm, ep in stages for cs in slabs]
    pending = None
    for mm, ep, cs in items:
        pre = mm(cs)
        if pending is not None:
            pending[0](pending[1], pending[2])
        pending = (ep, cs, pre)
    pending[0](pending[1], pending[2])

    ab = _dot(hb, wab_ref[...])
    lane = lax.broadcasted_iota(jnp.int32, ab.shape, 1)
    is_beta = lane < HEADS
    g_col = -jnp.exp(alog16_ref[...]) * _softplus(ab + dtb16_ref[...])
    g_col = jnp.where(is_beta, 0.0, g_col)
    beta_col = jax.nn.sigmoid(ab)
    abt = _dot_nt(wabt_ref[...], hb)
    g_row = -jnp.exp(alogc_ref[...]) * _softplus(abt[HEADS:2 * HEADS, :] + dtbc_ref[...])
    gc = GDN_CHUNK
    ri = lax.broadcasted_iota(jnp.int32, (gc, gc), 0)
    ci = lax.broadcasted_iota(jnp.int32, (gc, gc), 1)
    lower = jnp.where(ci <= ri, 1.0, 0.0)
    upper = jnp.where(ri <= ci, 1.0, 0.0)
    is_beta_c = lax.broadcasted_iota(jnp.int32, (gc, 2 * HEADS), 1) < HEADS
    for n in range(tm // gc):
        rs = slice(n * gc, (n + 1) * gc)
        gam_col = jnp.dot(lower, g_col[rs, :], preferred_element_type=F32,
                          precision=lax.Precision.HIGHEST)
        gcol_ref[0, rs, :] = jnp.where(is_beta_c, beta_col[rs, :], gam_col)
        grow_ref[0, :, rs] = jnp.dot(g_row[:, rs], upper, preferred_element_type=F32,
                                     precision=lax.Precision.HIGHEST)


def _in_proj(x, mod, n1g, wuv, wqkv, wz, wab, wabt, wg, convw, sw, sbt, vg, alog16, dtb16,
             alogc, dtbc, wa, cast_f32, layer):
    B, T, D = x.shape
    tm = TM_IN
    nsteps = B * (T // tm)
    tok = lambda b, t: (b, t, 0)
    bf_out = jax.ShapeDtypeStruct((B, T, D), BF16)
    consts = [n1g, wuv, wqkv, wz, wab, wabt, wg, convw, sw, sbt, vg, alog16, dtb16, alogc,
              dtbc, wa]
    return pl.pallas_call(
        _in_proj_body,
        grid=(B, T // tm),
        in_specs=[pl.BlockSpec((1, tm, D), tok),
                  pl.BlockSpec((1, 6, D), lambda b, t: (b, 0, 0))]
        + [_const_spec(a.shape) for a in consts]
        + [_cast_spec(a.shape[1:], nsteps, T // tm, layer) for a in cast_f32],
        out_specs=[pl.BlockSpec((1, tm, D), tok)] * 6
        + [pl.BlockSpec((1, tm, 2 * HEADS), tok),
           pl.BlockSpec((1, HEADS, tm), lambda b, t: (b, 0, t))]
        + [_cast_spec(a.shape[1:], nsteps, T // tm) for a in cast_f32],
        out_shape=[bf_out] * 6
        + [jax.ShapeDtypeStruct((B, T, 2 * HEADS), F32),
           jax.ShapeDtypeStruct((B, HEADS, T), F32)]
        + [jax.ShapeDtypeStruct(a.shape[1:], BF16) for a in cast_f32],
        scratch_shapes=[pltpu.VMEM((2, CONV_PAD, 3 * D), F32),
                        pltpu.VMEM((tm, D), BF16),
                        pltpu.VMEM((tm, D), F32)],
        compiler_params=pltpu.CompilerParams(
            dimension_semantics=("arbitrary", "arbitrary"),
            vmem_limit_bytes=V7X_VMEM_LIMIT),
        name="in_proj",
    )(x, mod, *consts, *cast_f32)


def _gdn_body(q_ref, k_ref, v_ref, zs_ref, gcol_ref, grow_ref, og_ref, y_ref, s_scr):
    c = GDN_CHUNK
    nch = q_ref.shape[1] // c
    hd = HEAD_DIM
    t = pl.program_id(1)

    @pl.when(t == 0)
    def _():
        s_scr[...] = jnp.zeros(s_scr.shape, F32)

    row = lax.broadcasted_iota(jnp.int32, (c, c), 0)
    col = lax.broadcasted_iota(jnp.int32, (c, c), 1)
    incl = col <= row
    strict = col < row
    nblk = c // INV_BLOCK
    sh = INV_BLOCK.bit_length() - 1
    lane_c = lax.broadcasted_iota(jnp.int32, (INV_BLOCK, c), 1)
    row_c = lax.broadcasted_iota(jnp.int32, (INV_BLOCK, c), 0)
    lane_blk = lane_c >> sh
    lane_in = lane_c & (INV_BLOCK - 1)
    spread = jnp.where((row >> sh) == (col >> sh), 1.0, 0.0).astype(BF16)
    eye_c = jnp.where(lane_in == row_c, 1.0, 0.0)
    merge_masks = []
    s = INV_BLOCK
    while s < c:
        b = s.bit_length() - 1
        merge_masks.append((((row >> b) & 1) == 1) & ((col >> b) == (row >> b) - 1))
        s *= 2

    hs = range(HEADS)
    sl = [slice(h * hd, (h + 1) * hd) for h in hs]
    rows = [slice(n * c, (n + 1) * c) for n in range(nch)]
    nb = q_ref.shape[0]
    ps = range(nb * nch * HEADS)
    pb = [p // (nch * HEADS) for p in ps]
    pn = [(p // HEADS) % nch for p in ps]
    ph = [p % HEADS for p in ps]
    gcs = [[gcol_ref[b, rows[n], :] for n in range(nch)] for b in range(nb)]
    grs = [[grow_ref[b, :, rows[n]] for n in range(nch)] for b in range(nb)]
    gcp = [gcs[pb[p]][pn[p]] for p in ps]
    qh = [q_ref[pb[p], rows[pn[p]], sl[ph[p]]] for p in ps]
    kh = [k_ref[pb[p], rows[pn[p]], sl[ph[p]]] for p in ps]
    beta = [gcp[p][:, ph[p]:ph[p] + 1] for p in ps]
    gam = [gcp[p][:, HEADS + ph[p]:HEADS + ph[p] + 1] for p in ps]
    g_last = [gcp[p][c - 1:c, HEADS + ph[p]:HEADS + ph[p] + 1] for p in ps]
    gam_r = [grs[pb[p]][pn[p]][ph[p]:ph[p] + 1, :] for p in ps]
    kb = [kh[p].astype(F32) * beta[p] for p in ps]
    kk = [_dot_nt(kb[p].astype(BF16), kh[p]) for p in ps]
    qk = [_dot_nt(qh[p], kh[p]) for p in ps]
    dec = [jnp.where(incl, jnp.exp(jnp.minimum(gam[p] - gam_r[p], 0.0)), 0.0) for p in ps]
    a_mat = [jnp.where(strict, kk[p] * dec[p], 0.0) for p in ps]
    qkd = [(qk[p] * dec[p]).astype(BF16) for p in ps]

    coef = []
    for p in ps:
        a_diag = jnp.zeros((INV_BLOCK, c), F32)
        for blk in range(nblk):
            a_diag = a_diag + jnp.where(
                lane_blk == blk, a_mat[p][blk * INV_BLOCK:(blk + 1) * INV_BLOCK, :], 0.0)
        picked = jnp.concatenate(
            [jnp.where(lane_in == j, a_diag, 0.0) for j in range(INV_BLOCK)], axis=0)
        coef.append(_dot(picked.astype(BF16), spread))
    x_inv = [eye_c for _ in ps]
    for j in range(INV_BLOCK - 1):
        x_inv = [x_inv[p] - coef[p][j * INV_BLOCK:(j + 1) * INV_BLOCK, :] * x_inv[p][j:j + 1, :]
                 for p in ps]
    t_mat = [jnp.concatenate([jnp.where(lane_blk == blk, x_inv[p], 0.0) for blk in range(nblk)],
                             axis=0) for p in ps]
    def odd_rows(m, bs):
        return jnp.concatenate([m[r0:r0 + bs, :] for r0 in range(bs, c, 2 * bs)], axis=0)

    def with_odd_rows(even, odd, bs):
        pieces = []
        for i, r0 in enumerate(range(0, c, 2 * bs)):
            pieces += [even[r0:r0 + bs, :], odd[i * bs:(i + 1) * bs, :]]
        return jnp.concatenate(pieces, axis=0)

    zeros_cc = jnp.zeros((c, c), BF16)
    bs = INV_BLOCK
    for mask in merge_masks:
        mask_odd = odd_rows(mask, bs)
        tb = [t_mat[p].astype(BF16) for p in ps]
        inner = [_dot(jnp.where(mask_odd, odd_rows(a_mat[p], bs), 0.0).astype(BF16), tb[p])
                 for p in ps]
        inner = [with_odd_rows(zeros_cc, inner[p].astype(BF16), bs) for p in ps]
        t_odd = [odd_rows(t_mat[p], bs) - _dot(odd_rows(tb[p], bs), inner[p]) for p in ps]
        t_mat = [with_odd_rows(t_mat[p], t_odd[p], bs) for p in ps]
        bs *= 2

    e_gam = [jnp.exp(gam[p]) for p in ps]
    rhs = [jnp.concatenate([v_ref[pb[p], rows[pn[p]], sl[ph[p]]].astype(F32) * beta[p],
                            kb[p] * e_gam[p]], axis=1).astype(BF16) for p in ps]
    uw = [_dot(t_mat[p].astype(BF16), rhs[p]) for p in ps]
    wq = [jnp.concatenate([uw[p][:, hd:], qh[p].astype(F32) * e_gam[p]], axis=0).astype(BF16)
          for p in ps]
    k_dec = [(kh[p].astype(F32) * jnp.exp(g_last[p] - gam[p])).astype(BF16) for p in ps]

    bh = [(b, h) for b in range(nb) for h in hs]
    ix = range(len(bh))
    s_cur = [s_scr[b, h] for b, h in bh]
    for n in range(nch):
        pp = [(b * nch + n) * HEADS + h for b, h in bh]
        ws = [_dot(wq[pp[i]], s_cur[i].astype(BF16)) for i in ix]
        v_new = [(uw[pp[i]][:, :hd] - ws[i][:c]).astype(BF16) for i in ix]
        o = [ws[i][c:] + _dot(qkd[pp[i]], v_new[i]) for i in ix]
        s_cur = [s_cur[i] * jnp.exp(g_last[pp[i]]) + _dot_tn(k_dec[pp[i]], v_new[i]) for i in ix]
        for i, (b, h) in enumerate(bh):
            on = o[i] * lax.rsqrt(jnp.mean(o[i] * o[i], axis=-1, keepdims=True) + NORM_EPS)
            y_ref[b, rows[n], sl[h]] = (
                on * og_ref[...] * zs_ref[b, rows[n], sl[h]].astype(F32)).astype(BF16)
    for i, (b, h) in enumerate(bh):
        s_scr[b, h] = s_cur[i]


def _gdn(q, k, v, zs, gcol, grow, og):
    B, T, D = q.shape
    c = GDN_CHUNK * GDN_STEP_CHUNKS
    nb = GDN_STEP_BATCH
    tok = lambda b, t: (b, t, 0)
    return pl.pallas_call(
        _gdn_body,
        grid=(B // nb, T // c),
        in_specs=[pl.BlockSpec((nb, c, D), tok)] * 4
        + [pl.BlockSpec((nb, c, 2 * HEADS), tok),
           pl.BlockSpec((nb, HEADS, c), lambda b, t: (b, 0, t)),
           _const_spec(og.shape)],
        out_specs=pl.BlockSpec((nb, c, D), tok),
        out_shape=jax.ShapeDtypeStruct((B, T, D), BF16),
        scratch_shapes=[pltpu.VMEM((nb, HEADS, HEAD_DIM, HEAD_DIM), F32)],
        compiler_params=pltpu.CompilerParams(
            dimension_semantics=("arbitrary", "arbitrary")),
        name="gdn",
    )(q, k, v, zs, gcol, grow, og)


def _out_ffn_body(x_ref, apart_ref, gb_ref, yb_ref, mod_ref, wb_ref, wo_ref, n2g_ref,
                  wfi_ref, wfo_ref, fg_ref, o_ref, *, final_norm, ffn_split):
    f = wfo_ref.shape[0]
    x = x_ref[0]
    merged = apart_ref[0].astype(F32) + gb_ref[0].astype(F32) * _dot(yb_ref[0], wb_ref[...])
    x = x + mod_ref[0, 2:3, :] * _dot(merged.astype(BF16), wo_ref[...])
    ms = jnp.mean(x * x, axis=-1, keepdims=True)
    h = x * lax.rsqrt(ms + NORM_EPS) * n2g_ref[...]
    h = h * (1.0 + mod_ref[0, 4:5, :]) + mod_ref[0, 3:4, :]
    hb = h.astype(BF16)
    fc = f // ffn_split

    def gate_up(j):
        return (_dot(hb, wfi_ref[:, j * fc:(j + 1) * fc]),
                _dot(hb, wfi_ref[:, f + j * fc:f + (j + 1) * fc]))

    nxt = gate_up(0)
    acc = None
    for j in range(ffn_split):
        gate, up = nxt
        if j + 1 < ffn_split:
            nxt = gate_up(j + 1)
        part = _dot((_silu(gate) * up).astype(BF16), wfo_ref[j * fc:(j + 1) * fc, :])
        acc = part if acc is None else acc + part
    x = x + mod_ref[0, 5:6, :] * acc
    if final_norm:
        ms = jnp.mean(x * x, axis=-1, keepdims=True)
        x = x * lax.rsqrt(ms + NORM_EPS) * fg_ref[...]
    o_ref[0] = x


def _out_ffn(x, apart, gb, yb, mod, wb, wo, n2g, wfi, wfo, fg, final_norm):
    B, T, D = x.shape
    tm = TM_OUT
    tok = lambda b, t: (b, t, 0)
    consts = [wb, wo, n2g, wfi, wfo, fg]
    return pl.pallas_call(
        functools.partial(_out_ffn_body, final_norm=final_norm, ffn_split=FFN_SPLIT),
        grid=(B, T // tm),
        in_specs=[pl.BlockSpec((1, tm, D), tok)] * 4
        + [pl.BlockSpec((1, 6, D), lambda b, t: (b, 0, 0))]
        + [_const_spec(a.shape) for a in consts],
        out_specs=pl.BlockSpec((1, tm, D), tok),
        out_shape=jax.ShapeDtypeStruct((B, T, D), F32),
        compiler_params=pltpu.CompilerParams(
            dimension_semantics=("arbitrary", "arbitrary"),
            vmem_limit_bytes=V7X_VMEM_LIMIT),
        name="out_ffn",
    )(x, apart, gb, yb, mod, *consts)


def kernel(x, c, ada_w, ada_b, norm1_g, w_in, conv_w, spatial_w, spatial_b, v_norm_g, a_log,
           dt_bias, o_norm_g, w_branch_a, w_branch_b, w_out, norm2_g, w_ffn_in, w_ffn_out,
           final_g):
    B, T, D = x.shape
    L = ada_w.shape[0]
    assert D == A_GROUPS * A_CHUNK == HEADS * HEAD_DIM
    assert T % TM_IN == 0 and T % TM_OUT == 0 and TM_IN % GDN_CHUNK == 0
    assert B % GDN_STEP_BATCH == 0 and T % (GDN_CHUNK * GDN_STEP_CHUNKS) == 0
    assert conv_w.shape[1] == CONV_K

    mod_all = _ada_mod(c, ada_w, ada_b).reshape(L, B, 6, D)
    o_uv, o_qkv, o_z, o_ab, o_g = 0, 2 * D, 5 * D, 6 * D, 6 * D + 2 * HEADS
    zeros8 = jnp.zeros((HEADS,), F32)
    for i in range(L):
        wi = w_in[i]
        wab = wi[:, o_ab:o_g].astype(BF16)
        apart, gb, q, k, v, zs, gcol, grow, wb, wo, wfi, wfo = _in_proj(
            x, mod_all[i], norm1_g[i].reshape(1, D),
            wi[:, o_uv:o_qkv].astype(BF16), wi[:, o_qkv:o_z].astype(BF16),
            wi[:, o_z:o_ab].astype(BF16), wab, wab.T, wi[:, o_g:].astype(BF16),
            conv_w[i], spatial_w[i], spatial_b[i].T, v_norm_g[i].reshape(1, D),
            jnp.concatenate([zeros8, a_log[i]]).reshape(1, 2 * HEADS),
            jnp.concatenate([zeros8, dt_bias[i]]).reshape(1, 2 * HEADS),
            a_log[i].reshape(HEADS, 1), dt_bias[i].reshape(HEADS, 1),
            w_branch_a[i].astype(BF16), [w_branch_b, w_out, w_ffn_in, w_ffn_out], i)
        yb = _gdn(q, k, v, zs, gcol, grow, o_norm_g[i].reshape(1, HEAD_DIM))
        x = _out_ffn(x, apart, gb, yb, mod_all[i], wb, wo, norm2_g[i].reshape(1, D), wfi, wfo,
                     final_g.reshape(1, D), final_norm=(i == L - 1))
    return x
```

```python
import functools
import math

import jax
import jax.numpy as jnp
from jax import lax
from jax.experimental import pallas as pl
from jax.experimental.pallas import tpu as pltpu

NORM_EPS = 1e-6
A_GROUPS = 8
A_CHUNK = 128
HEADS = 8
HEAD_DIM = 128
CONV_K = 4
GDN_CHUNK = 128
INV_BLOCK = 16
BF16_SUBLANES = 16
CONV_PAD = 8

V7X_VMEM_LIMIT = 56 * 1024 * 1024

TM_IN = 512
TM_OUT = 512
GDN_STEP_CHUNKS = 2
GDN_STEP_BATCH = 2
IN_SLAB = 256
FFN_SPLIT = 11

BF16 = jnp.bfloat16
F32 = jnp.float32
LOG2E = 1.4426950408889634


def _dot(a, b):
    return jnp.dot(a, b, preferred_element_type=F32)


def _dot_nt(a, b):
    return lax.dot_general(a, b, (((1,), (1,)), ((), ())), preferred_element_type=F32)


def _dot_tn(a, b):
    return lax.dot_general(a, b, (((0,), (0,)), ((), ())), preferred_element_type=F32)


def _softplus(x):
    return jnp.maximum(x, 0.0) + jnp.log1p(jnp.exp(-jnp.abs(x)))


def _gelu_tanh(x):
    a = -2.0 * 0.7978845608028654 * LOG2E
    return x / (1.0 + jnp.exp2(x * (a + (a * 0.044715) * (x * x))))


def _silu(x):
    return x * jax.nn.sigmoid(x)


def _const_spec(shape):
    nd = len(shape)
    return pl.BlockSpec(shape, lambda *_: (0,) * nd, pipeline_mode=pl.Buffered(1))


def _cast_blocks(rows, nsteps):
    nblk = math.gcd(rows // BF16_SUBLANES, nsteps)
    return rows // nblk, nsteps // nblk


def _cast_spec(shape, nsteps, nt, layer=None):
    rb, rep = _cast_blocks(shape[0], nsteps)
    if layer is None:
        return pl.BlockSpec((rb, shape[1]), lambda b, t: ((b * nt + t) // rep, 0))
    return pl.BlockSpec((None, rb, shape[1]), lambda b, t: (layer, (b * nt + t) // rep, 0))


def _ada_body(c_ref, w_ref, b_ref, o_ref):
    cond = _silu(c_ref[...]).astype(BF16)
    o_ref[0] = _dot(cond, w_ref[0].astype(BF16)) + b_ref[0]


def _ada_mod(c, ada_w, ada_b):
    L, D, D6 = ada_w.shape
    B = c.shape[0]
    nblk = D6 // D
    return pl.pallas_call(
        _ada_body,
        grid=(L, nblk),
        in_specs=[
            pl.BlockSpec((B, D), lambda l, j: (0, 0)),
            pl.BlockSpec((1, D, D), lambda l, j: (l, 0, j)),
            pl.BlockSpec((1, 1, D), lambda l, j: (l, 0, j)),
        ],
        out_specs=pl.BlockSpec((1, B, D), lambda l, j: (l, 0, j)),
        out_shape=jax.ShapeDtypeStruct((L, B, D6), F32),
        name="ada_mod",
    )(c, ada_w, ada_b.reshape(L, 1, D6))


def _in_proj_body(x_ref, mod_ref, n1g_ref, wuv_ref, wqkv_ref, wz_ref, wab_ref, wabt_ref,
                  wg_ref, convw_ref, sw_ref, sbt_ref, vg_ref, alog16_ref, dtb16_ref,
                  alogc_ref, dtbc_ref, wa_ref, wb_f, wo_f, wfi_f, wfo_f,
                  apart_ref, gb_ref, q_ref, k_ref, v_ref, zs_ref, gcol_ref, grow_ref,
                  wb_o, wo_o, wfi_o, wfo_o,
                  carry_scr, ya_scr, u_scr):
    tm, d = x_ref.shape[1], x_ref.shape[2]
    t = pl.program_id(1)

    @pl.when(t == 0)
    def _():
        carry_scr[...] = jnp.zeros(carry_scr.shape, F32)

    for src, dst in ((wb_f, wb_o), (wo_f, wo_o), (wfi_f, wfi_o), (wfo_f, wfo_o)):
        dst[...] = src[...].astype(BF16)

    x = x_ref[0]
    ms = jnp.mean(x * x, axis=-1, keepdims=True)
    h = x * lax.rsqrt(ms + NORM_EPS) * n1g_ref[...]
    h = h * (1.0 + mod_ref[0, 1:2, :]) + mod_ref[0, 0:1, :]
    hb = h.astype(BF16)

    hd = HEAD_DIM
    gd = d // A_GROUPS
    sw_w = IN_SLAB
    row = lax.broadcasted_iota(jnp.int32, (A_CHUNK, A_CHUNK), 0)
    col = lax.broadcasted_iota(jnp.int32, (A_CHUNK, A_CHUNK), 1)
    causal = col <= row

    def ep_u(cs, pre):
        u_scr[:, cs] = _gelu_tanh(pre)

    def ep_v(cs, pre):
        v = _gelu_tanh(pre)
        for gl in range(sw_w // gd):
            g = cs.start // gd + gl
            gs = slice(g * gd, (g + 1) * gd)
            vg = v[:, gl * gd:(gl + 1) * gd]
            vn = vg * lax.rsqrt(jnp.mean(vg * vg, axis=-1, keepdims=True) + NORM_EPS)
            vn = (vn * vg_ref[:, gs]).astype(BF16)
            wm = jnp.where(causal, sw_ref[g], 0.0).astype(BF16)
            bias = sbt_ref[:, g:g + 1]
            for n in range(tm // A_CHUNK):
                rs = slice(n * A_CHUNK, (n + 1) * A_CHUNK)
                s = _dot(wm, vn[rs, :]) + bias
                ya_scr[rs, gs] = (u_scr[rs, gs] * s).astype(BF16)

    def ep_gb(cs, pre):
        gb_ref[0, :, cs] = jax.nn.sigmoid(pre).astype(BF16)

    def ep_apart(cs, pre):
        d_ga, d_a = pre
        apart_ref[0, :, cs] = (jax.nn.sigmoid(d_ga) * d_a).astype(BF16)

    row8 = lax.broadcasted_iota(jnp.int32, (CONV_PAD, sw_w), 0)

    def shift_rows(a, prev_tail, k):
        r = pltpu.roll(a, k, axis=0)
        head = jnp.where(row8 < k, pltpu.roll(prev_tail, k, axis=0), r[0:CONV_PAD, :])
        return jnp.concatenate([head, r[CONV_PAD:, :]], axis=0)

    def conv_silu(blk, cs, pre):
        bs = slice(blk * d + cs.start, blk * d + cs.stop)
        x1 = shift_rows(pre, carry_scr[0, :, bs], 1)
        z = pre * convw_ref[1:2, bs] + x1 * convw_ref[0:1, bs]
        acc = (pre * convw_ref[3:4, bs] + x1 * convw_ref[2:3, bs]
               + shift_rows(z, carry_scr[1, :, bs], 2))
        carry_scr[0, :, bs] = pre[tm - CONV_PAD:tm, :]
        carry_scr[1, :, bs] = z[tm - CONV_PAD:tm, :]
        return _silu(acc)

    def l2_heads(cs, act, out_ref, scale):
        for hl in range(sw_w // hd):
            ah = act[:, hl * hd:(hl + 1) * hd]
            inv = lax.rsqrt(jnp.sum(ah * ah, axis=-1, keepdims=True) + NORM_EPS) * scale
            out_ref[0, :, cs.start + hl * hd:cs.start + (hl + 1) * hd] = (ah * inv).astype(BF16)

    def ep_q(cs, pre):
        l2_heads(cs, conv_silu(0, cs, pre), q_ref, hd ** -0.5)

    def ep_k(cs, pre):
        l2_heads(cs, conv_silu(1, cs, pre), k_ref, 1.0)

    def ep_vv(cs, pre):
        v_ref[0, :, cs] = conv_silu(2, cs, pre).astype(BF16)

    def ep_z(cs, pre):
        zs_ref[0, :, cs] = _silu(pre).astype(BF16)

    slabs = [slice(c0, c0 + sw_w) for c0 in range(0, d, sw_w)]

    def proj(w_ref, off):
        return lambda cs: _dot(hb, w_ref[:, off + cs.start:off + cs.stop])

    def proj_apart(cs):
        return (_dot(hb, wg_ref[:, cs]), _dot(ya_scr[...], wa_ref[:, cs]))

    stages = [(proj(wuv_ref, 0), ep_u), (proj(wuv_ref, d), ep_v), (proj(wg_ref, d), ep_gb),
              (proj(wqkv_ref, 0), ep_q), (proj_apart, ep_apart), (proj(wqkv_ref, d), ep_k),
              (proj(wqkv_ref, 2 * d), ep_vv), (proj(wz_ref, 0), ep_z)]
    items = [(mm, ep, cs) for mm, ep in stages for cs in slabs]
    pending = None
    for mm, ep, cs in items:
        pre = mm(cs)
        if pending is not None:
            pending[0](pending[1], pending[2])
        pending = (ep, cs, pre)
    pending[0](pending[1], pending[2])

    ab = _dot(hb, wab_ref[...])
    lane = lax.broadcasted_iota(jnp.int32, ab.shape, 1)
    is_beta = lane < HEADS
    g_col = -jnp.exp(alog16_ref[...]) * _softplus(ab + dtb16_ref[...])
    g_col = jnp.where(is_beta, 0.0, g_col)
    beta_col = jax.nn.sigmoid(ab)
    abt = _dot_nt(wabt_ref[...], hb)
    g_row = -jnp.exp(alogc_ref[...]) * _softplus(abt[HEADS:2 * HEADS, :] + dtbc_ref[...])
    gc = GDN_CHUNK
    ri = lax.broadcasted_iota(jnp.int32, (gc, gc), 0)
    ci = lax.broadcasted_iota(jnp.int32, (gc, gc), 1)
    lower = jnp.where(ci <= ri, 1.0, 0.0)
    upper = jnp.where(ri <= ci, 1.0, 0.0)
    is_beta_c = lax.broadcasted_iota(jnp.int32, (gc, 2 * HEADS), 1) < HEADS
    for n in range(tm // gc):
        rs = slice(n * gc, (n + 1) * gc)
        gam_col = jnp.dot(lower, g_col[rs, :], preferred_element_type=F32,
                          precision=lax.Precision.HIGHEST)
        gcol_ref[0, rs, :] = jnp.where(is_beta_c, beta_col[rs, :], gam_col)
        grow_ref[0, :, rs] = jnp.dot(g_row[:, rs], upper, preferred_element_type=F32,
                                     precision=lax.Precision.HIGHEST)


def _in_proj(x, mod, n1g, wuv, wqkv, wz, wab, wabt, wg, convw, sw, sbt, vg, alog16, dtb16,
             alogc, dtbc, wa, cast_f32, layer):
    B, T, D = x.shape
    tm = TM_IN
    nsteps = B * (T // tm)
    tok = lambda b, t: (b, t, 0)
    bf_out = jax.ShapeDtypeStruct((B, T, D), BF16)
    consts = [n1g, wuv, wqkv, wz, wab, wabt, wg, convw, sw, sbt, vg, alog16, dtb16, alogc,
              dtbc, wa]
    return pl.pallas_call(
        _in_proj_body,
        grid=(B, T // tm),
        in_specs=[pl.BlockSpec((1, tm, D), tok),
                  pl.BlockSpec((1, 6, D), lambda b, t: (b, 0, 0))]
        + [_const_spec(a.shape) for a in consts]
        + [_cast_spec(a.shape[1:], nsteps, T // tm, layer) for a in cast_f32],
        out_specs=[pl.BlockSpec((1, tm, D), tok)] * 6
        + [pl.BlockSpec((1, tm, 2 * HEADS), tok),
           pl.BlockSpec((1, HEADS, tm), lambda b, t: (b, 0, t))]
        + [_cast_spec(a.shape[1:], nsteps, T // tm) for a in cast_f32],
        out_shape=[bf_out] * 6
        + [jax.ShapeDtypeStruct((B, T, 2 * HEADS), F32),
           jax.ShapeDtypeStruct((B, HEADS, T), F32)]
        + [jax.ShapeDtypeStruct(a.shape[1:], BF16) for a in cast_f32],
        scratch_shapes=[pltpu.VMEM((2, CONV_PAD, 3 * D), F32),
                        pltpu.VMEM((tm, D), BF16),
                        pltpu.VMEM((tm, D), F32)],
        compiler_params=pltpu.CompilerParams(
            dimension_semantics=("arbitrary", "arbitrary"),
            vmem_limit_bytes=V7X_VMEM_LIMIT),
        name="in_proj",
    )(x, mod, *consts, *cast_f32)


def _gdn_body(q_ref, k_ref, v_ref, zs_ref, gcol_ref, grow_ref, og_ref, y_ref, s_scr):
    c = GDN_CHUNK
    nch = q_ref.shape[1] // c
    hd = HEAD_DIM
    t = pl.program_id(1)

    @pl.when(t == 0)
    def _():
        s_scr[...] = jnp.zeros(s_scr.shape, F32)

    row = lax.broadcasted_iota(jnp.int32, (c, c), 0)
    col = lax.broadcasted_iota(jnp.int32, (c, c), 1)
    incl = col <= row
    strict = col < row
    nblk = c // INV_BLOCK
    sh = INV_BLOCK.bit_length() - 1
    lane_c = lax.broadcasted_iota(jnp.int32, (INV_BLOCK, c), 1)
    row_c = lax.broadcasted_iota(jnp.int32, (INV_BLOCK, c), 0)
    lane_blk = lane_c >> sh
    lane_in = lane_c & (INV_BLOCK - 1)
    spread = jnp.where((row >> sh) == (col >> sh), 1.0, 0.0).astype(BF16)
    eye_c = jnp.where(lane_in == row_c, 1.0, 0.0)
    merge_masks = []
    s = INV_BLOCK
    while s < c:
        b = s.bit_length() - 1
        merge_masks.append((((row >> b) & 1) == 1) & ((col >> b) == (row >> b) - 1))
        s *= 2

    hs = range(HEADS)
    sl = [slice(h * hd, (h + 1) * hd) for h in hs]
    rows = [slice(n * c, (n + 1) * c) for n in range(nch)]
    nb = q_ref.shape[0]
    ps = range(nb * nch * HEADS)
    pb = [p // (nch * HEADS) for p in ps]
    pn = [(p // HEADS) % nch for p in ps]
    ph = [p % HEADS for p in ps]
    gcs = [[gcol_ref[b, rows[n], :] for n in range(nch)] for b in range(nb)]
    grs = [[grow_ref[b, :, rows[n]] for n in range(nch)] for b in range(nb)]
    gcp = [gcs[pb[p]][pn[p]] for p in ps]
    qh = [q_ref[pb[p], rows[pn[p]], sl[ph[p]]] for p in ps]
    kh = [k_ref[pb[p], rows[pn[p]], sl[ph[p]]] for p in ps]
    beta = [gcp[p][:, ph[p]:ph[p] + 1] for p in ps]
    gam = [gcp[p][:, HEADS + ph[p]:HEADS + ph[p] + 1] for p in ps]
    g_last = [gcp[p][c - 1:c, HEADS + ph[p]:HEADS + ph[p] + 1] for p in ps]
    gam_r = [grs[pb[p]][pn[p]][ph[p]:ph[p] + 1, :] for p in ps]
    kb = [kh[p].astype(F32) * beta[p] for p in ps]
    kq = [_dot_nt(jnp.concatenate([kb[p].astype(BF16), qh[p]], axis=0), kh[p]) for p in ps]
    kk = [kq[p][:c] for p in ps]
    qk = [kq[p][c:] for p in ps]
    dec = [jnp.where(incl, jnp.exp(jnp.minimum(gam[p] - gam_r[p], 0.0)), 0.0) for p in ps]
    a_mat = [jnp.where(strict, kk[p] * dec[p], 0.0) for p in ps]
    qkd = [(qk[p] * dec[p]).astype(BF16) for p in ps]

    coef = []
    for p in ps:
        a_diag = jnp.zeros((INV_BLOCK, c), F32)
        for blk in range(nblk):
            a_diag = a_diag + jnp.where(
                lane_blk == blk, a_mat[p][blk * INV_BLOCK:(blk + 1) * INV_BLOCK, :], 0.0)
        picked = jnp.concatenate(
            [jnp.where(lane_in == j, a_diag, 0.0) for j in range(INV_BLOCK)], axis=0)
        coef.append(_dot(picked.astype(BF16), spread))
    x_inv = [eye_c for _ in ps]
    for j in range(INV_BLOCK - 1):
        x_inv = [x_inv[p] - coef[p][j * INV_BLOCK:(j + 1) * INV_BLOCK, :] * x_inv[p][j:j + 1, :]
                 for p in ps]
    t_mat = [jnp.concatenate([jnp.where(lane_blk == blk, x_inv[p], 0.0) for blk in range(nblk)],
                             axis=0) for p in ps]
    def odd_rows(m, bs):
        return jnp.concatenate([m[r0:r0 + bs, :] for r0 in range(bs, c, 2 * bs)], axis=0)

    def with_odd_rows(even, odd, bs):
        pieces = []
        for i, r0 in enumerate(range(0, c, 2 * bs)):
            pieces += [even[r0:r0 + bs, :], odd[i * bs:(i + 1) * bs, :]]
        return jnp.concatenate(pieces, axis=0)

    zeros_cc = jnp.zeros((c, c), BF16)
    bs = INV_BLOCK
    for mask in merge_masks:
        mask_odd = odd_rows(mask, bs)
        tb = [t_mat[p].astype(BF16) for p in ps]
        inner = [_dot(jnp.where(mask_odd, odd_rows(a_mat[p], bs), 0.0).astype(BF16), tb[p])
                 for p in ps]
        inner = [with_odd_rows(zeros_cc, inner[p].astype(BF16), bs) for p in ps]
        t_odd = [odd_rows(t_mat[p], bs) - _dot(odd_rows(tb[p], bs), inner[p]) for p in ps]
        t_mat = [with_odd_rows(t_mat[p], t_odd[p], bs) for p in ps]
        bs *= 2

    e_gam = [jnp.exp(gam[p]) for p in ps]
    rhs = [jnp.concatenate([v_ref[pb[p], rows[pn[p]], sl[ph[p]]].astype(F32) * beta[p],
                            kb[p] * e_gam[p]], axis=1).astype(BF16) for p in ps]
    uw = [_dot(t_mat[p].astype(BF16), rhs[p]) for p in ps]
    wq = [jnp.concatenate([uw[p][:, hd:], qh[p].astype(F32) * e_gam[p]], axis=0).astype(BF16)
          for p in ps]
    k_dec = [(kh[p].astype(F32) * jnp.exp(g_last[p] - gam[p])).astype(BF16) for p in ps]

    bh = [(b, h) for b in range(nb) for h in hs]
    ix = range(len(bh))
    s_cur = [s_scr[b, h] for b, h in bh]
    for n in range(nch):
        pp = [(b * nch + n) * HEADS + h for b, h in bh]
        sb = [s_cur[i].astype(BF16) for i in ix]
        v_new = [(uw[pp[i]][:, :hd] - _dot(wq[pp[i]][:c], sb[i])).astype(BF16) for i in ix]
        o = [_dot(jnp.concatenate([wq[pp[i]][c:], qkd[pp[i]]], axis=1),
                  jnp.concatenate([sb[i], v_new[i]], axis=0)) for i in ix]
        s_cur = [s_cur[i] * jnp.exp(g_last[pp[i]]) + _dot_tn(k_dec[pp[i]], v_new[i]) for i in ix]
        for i, (b, h) in enumerate(bh):
            on = o[i] * lax.rsqrt(jnp.mean(o[i] * o[i], axis=-1, keepdims=True) + NORM_EPS)
            y_ref[b, rows[n], sl[h]] = (
                on * og_ref[...] * zs_ref[b, rows[n], sl[h]].astype(F32)).astype(BF16)
    for i, (b, h) in enumerate(bh):
        s_scr[b, h] = s_cur[i]


def _gdn(q, k, v, zs, gcol, grow, og):
    B, T, D = q.shape
    c = GDN_CHUNK * GDN_STEP_CHUNKS
    nb = GDN_STEP_BATCH
    tok = lambda b, t: (b, t, 0)
    return pl.pallas_call(
        _gdn_body,
        grid=(B // nb, T // c),
        in_specs=[pl.BlockSpec((nb, c, D), tok)] * 4
        + [pl.BlockSpec((nb, c, 2 * HEADS), tok),
           pl.BlockSpec((nb, HEADS, c), lambda b, t: (b, 0, t)),
           _const_spec(og.shape)],
        out_specs=pl.BlockSpec((nb, c, D), tok),
        out_shape=jax.ShapeDtypeStruct((B, T, D), BF16),
        scratch_shapes=[pltpu.VMEM((nb, HEADS, HEAD_DIM, HEAD_DIM), F32)],
        compiler_params=pltpu.CompilerParams(
            dimension_semantics=("arbitrary", "arbitrary")),
        name="gdn",
    )(q, k, v, zs, gcol, grow, og)


def _out_ffn_body(x_ref, apart_ref, gb_ref, yb_ref, mod_ref, wb_ref, wo_ref, n2g_ref,
                  wfi_ref, wfo_ref, fg_ref, o_ref, *, final_norm, ffn_split):
    f = wfo_ref.shape[0]
    x = x_ref[0]
    merged = apart_ref[0].astype(F32) + gb_ref[0].astype(F32) * _dot(yb_ref[0], wb_ref[...])
    x = x + mod_ref[0, 2:3, :] * _dot(merged.astype(BF16), wo_ref[...])
    ms = jnp.mean(x * x, axis=-1, keepdims=True)
    h = x * lax.rsqrt(ms + NORM_EPS) * n2g_ref[...]
    h = h * (1.0 + mod_ref[0, 4:5, :]) + mod_ref[0, 3:4, :]
    hb = h.astype(BF16)
    fc = f // ffn_split

    def gate_up(j):
        return (_dot(hb, wfi_ref[:, j * fc:(j + 1) * fc]),
                _dot(hb, wfi_ref[:, f + j * fc:f + (j + 1) * fc]))

    nxt = gate_up(0)
    acc = None
    for j in range(ffn_split):
        gate, up = nxt
        if j + 1 < ffn_split:
            nxt = gate_up(j + 1)
        part = _dot((_silu(gate) * up).astype(BF16), wfo_ref[j * fc:(j + 1) * fc, :])
        acc = part if acc is None else acc + part
    x = x + mod_ref[0, 5:6, :] * acc
    if final_norm:
        ms = jnp.mean(x * x, axis=-1, keepdims=True)
        x = x * lax.rsqrt(ms + NORM_EPS) * fg_ref[...]
    o_ref[0] = x


def _out_ffn(x, apart, gb, yb, mod, wb, wo, n2g, wfi, wfo, fg, final_norm):
    B, T, D = x.shape
    tm = TM_OUT
    tok = lambda b, t: (b, t, 0)
    consts = [wb, wo, n2g, wfi, wfo, fg]
    return pl.pallas_call(
        functools.partial(_out_ffn_body, final_norm=final_norm, ffn_split=FFN_SPLIT),
        grid=(B, T // tm),
        in_specs=[pl.BlockSpec((1, tm, D), tok)] * 4
        + [pl.BlockSpec((1, 6, D), lambda b, t: (b, 0, 0))]
        + [_const_spec(a.shape) for a in consts],
        out_specs=pl.BlockSpec((1, tm, D), tok),
        out_shape=jax.ShapeDtypeStruct((B, T, D), F32),
        compiler_params=pltpu.CompilerParams(
            dimension_semantics=("arbitrary", "arbitrary"),
            vmem_limit_bytes=V7X_VMEM_LIMIT),
        name="out_ffn",
    )(x, apart, gb, yb, mod, *consts)


def kernel(x, c, ada_w, ada_b, norm1_g, w_in, conv_w, spatial_w, spatial_b, v_norm_g, a_log,
           dt_bias, o_norm_g, w_branch_a, w_branch_b, w_out, norm2_g, w_ffn_in, w_ffn_out,
           final_g):
    B, T, D = x.shape
    L = ada_w.shape[0]
    assert D == A_GROUPS * A_CHUNK == HEADS * HEAD_DIM
    assert T % TM_IN == 0 and T % TM_OUT == 0 and TM_IN % GDN_CHUNK == 0
    assert B % GDN_STEP_BATCH == 0 and T % (GDN_CHUNK * GDN_STEP_CHUNKS) == 0
    assert conv_w.shape[1] == CONV_K

    mod_all = _ada_mod(c, ada_w, ada_b).reshape(L, B, 6, D)
    o_uv, o_qkv, o_z, o_ab, o_g = 0, 2 * D, 5 * D, 6 * D, 6 * D + 2 * HEADS
    zeros8 = jnp.zeros((HEADS,), F32)
    for i in range(L):
        wi = w_in[i]
        wab = wi[:, o_ab:o_g].astype(BF16)
        apart, gb, q, k, v, zs, gcol, grow, wb, wo, wfi, wfo = _in_proj(
            x, mod_all[i], norm1_g[i].reshape(1, D),
            wi[:, o_uv:o_qkv].astype(BF16), wi[:, o_qkv:o_z].astype(BF16),
            wi[:, o_z:o_ab].astype(BF16), wab, wab.T, wi[:, o_g:].astype(BF16),
            conv_w[i], spatial_w[i], spatial_b[i].T, v_norm_g[i].reshape(1, D),
            jnp.concatenate([zeros8, a_log[i]]).reshape(1, 2 * HEADS),
            jnp.concatenate([zeros8, dt_bias[i]]).reshape(1, 2 * HEADS),
            a_log[i].reshape(HEADS, 1), dt_bias[i].reshape(HEADS, 1),
            w_branch_a[i].astype(BF16), [w_branch_b, w_out, w_ffn_in, w_ffn_out], i)
        yb = _gdn(q, k, v, zs, gcol, grow, o_norm_g[i].reshape(1, HEAD_DIM))
        x = _out_ffn(x, apart, gb, yb, mod_all[i], wb, wo, norm2_g[i].reshape(1, D), wfi, wfo,
                     final_g.reshape(1, D), final_norm=(i == L - 1))
    return x
```

```python
import functools
import math

import jax
import jax.numpy as jnp
from jax import lax
from jax.experimental import pallas as pl
from jax.experimental.pallas import tpu as pltpu

NORM_EPS = 1e-6
A_GROUPS = 8
A_CHUNK = 128
HEADS = 8
HEAD_DIM = 128
CONV_K = 4
GDN_CHUNK = 128
INV_BLOCK = 16
BF16_SUBLANES = 16
CONV_PAD = 8

V7X_VMEM_LIMIT = 56 * 1024 * 1024

TM_IN = 512
TM_OUT = 512
GDN_STEP_CHUNKS = 2
GDN_STEP_BATCH = 2
IN_SLAB = 256
ADA_BLOCK = 2048
FFN_SPLIT = 11

BF16 = jnp.bfloat16
F32 = jnp.float32
LOG2E = 1.4426950408889634


def _dot(a, b):
    return jnp.dot(a, b, preferred_element_type=F32)


def _dot_nt(a, b):
    return lax.dot_general(a, b, (((1,), (1,)), ((), ())), preferred_element_type=F32)


def _dot_tn(a, b):
    return lax.dot_general(a, b, (((0,), (0,)), ((), ())), preferred_element_type=F32)


def _softplus(x):
    return jnp.maximum(x, 0.0) + jnp.log1p(jnp.exp(-jnp.abs(x)))


def _gelu_tanh(x):
    a = -2.0 * 0.7978845608028654 * LOG2E
    return x / (1.0 + jnp.exp2(x * (a + (a * 0.044715) * (x * x))))


def _silu(x):
    return x * jax.nn.sigmoid(x)


def _const_spec(shape):
    nd = len(shape)
    return pl.BlockSpec(shape, lambda *_: (0,) * nd, pipeline_mode=pl.Buffered(1))


def _cast_blocks(rows, nsteps):
    nblk = math.gcd(rows // BF16_SUBLANES, nsteps)
    return rows // nblk, nsteps // nblk


def _cast_spec(shape, nsteps, nt, layer=None):
    rb, rep = _cast_blocks(shape[0], nsteps)
    if layer is None:
        return pl.BlockSpec((rb, shape[1]), lambda b, t: ((b * nt + t) // rep, 0))
    return pl.BlockSpec((None, rb, shape[1]), lambda b, t: (layer, (b * nt + t) // rep, 0))


def _ada_body(c_ref, w_ref, b_ref, o_ref):
    cond = _silu(c_ref[...]).astype(BF16)
    o_ref[0] = _dot(cond, w_ref[0].astype(BF16)) + b_ref[0]


def _ada_mod(c, ada_w, ada_b):
    L, D, D6 = ada_w.shape
    B = c.shape[0]
    nblk = D6 // ADA_BLOCK
    return pl.pallas_call(
        _ada_body,
        grid=(L, nblk),
        in_specs=[
            pl.BlockSpec((B, D), lambda l, j: (0, 0)),
            pl.BlockSpec((1, D, ADA_BLOCK), lambda l, j: (l, 0, j)),
            pl.BlockSpec((1, 1, ADA_BLOCK), lambda l, j: (l, 0, j)),
        ],
        out_specs=pl.BlockSpec((1, B, ADA_BLOCK), lambda l, j: (l, 0, j)),
        out_shape=jax.ShapeDtypeStruct((L, B, D6), F32),
        name="ada_mod",
    )(c, ada_w, ada_b.reshape(L, 1, D6))


def _in_proj_body(x_ref, mod_ref, n1g_ref, wuv_ref, wqkv_ref, wz_ref, wab_ref, wabt_ref,
                  wg_ref, convw_ref, sw_ref, sbt_ref, vg_ref, alog16_ref, dtb16_ref,
                  alogc_ref, dtbc_ref, wa_ref, wb_f, wo_f, wfi_f, wfo_f,
                  apart_ref, gb_ref, q_ref, k_ref, v_ref, zs_ref, gcol_ref, grow_ref,
                  wb_o, wo_o, wfi_o, wfo_o,
                  carry_scr, ya_scr, u_scr):
    tm, d = x_ref.shape[1], x_ref.shape[2]
    t = pl.program_id(1)

    @pl.when(t == 0)
    def _():
        carry_scr[...] = jnp.zeros(carry_scr.shape, F32)

    for src, dst in ((wb_f, wb_o), (wo_f, wo_o), (wfi_f, wfi_o), (wfo_f, wfo_o)):
        dst[...] = src[...].astype(BF16)

    x = x_ref[0]
    ms = jnp.mean(x * x, axis=-1, keepdims=True)
    h = x * lax.rsqrt(ms + NORM_EPS) * n1g_ref[...]
    h = h * (1.0 + mod_ref[0, 1:2, :]) + mod_ref[0, 0:1, :]
    hb = h.astype(BF16)

    hd = HEAD_DIM
    gd = d // A_GROUPS
    sw_w = IN_SLAB
    row = lax.broadcasted_iota(jnp.int32, (A_CHUNK, A_CHUNK), 0)
    col = lax.broadcasted_iota(jnp.int32, (A_CHUNK, A_CHUNK), 1)
    causal = col <= row

    def ep_u(cs, pre):
        u_scr[:, cs] = _gelu_tanh(pre)

    def ep_v(cs, pre):
        v = _gelu_tanh(pre)
        for gl in range(sw_w // gd):
            g = cs.start // gd + gl
            gs = slice(g * gd, (g + 1) * gd)
            vg = v[:, gl * gd:(gl + 1) * gd]
            vn = vg * lax.rsqrt(jnp.mean(vg * vg, axis=-1, keepdims=True) + NORM_EPS)
            vn = (vn * vg_ref[:, gs]).astype(BF16)
            wm = jnp.where(causal, sw_ref[g], 0.0).astype(BF16)
            bias = sbt_ref[:, g:g + 1]
            for n in range(tm // A_CHUNK):
                rs = slice(n * A_CHUNK, (n + 1) * A_CHUNK)
                s = _dot(wm, vn[rs, :]) + bias
                ya_scr[rs, gs] = (u_scr[rs, gs] * s).astype(BF16)

    def ep_gb(cs, pre):
        gb_ref[0, :, cs] = jax.nn.sigmoid(pre).astype(BF16)

    def ep_apart(cs, pre):
        d_ga, d_a = pre
        apart_ref[0, :, cs] = (jax.nn.sigmoid(d_ga) * d_a).astype(BF16)

    row8 = lax.broadcasted_iota(jnp.int32, (CONV_PAD, sw_w), 0)

    def shift_rows(a, prev_tail, k):
        r = pltpu.roll(a, k, axis=0)
        head = jnp.where(row8 < k, pltpu.roll(prev_tail, k, axis=0), r[0:CONV_PAD, :])
        return jnp.concatenate([head, r[CONV_PAD:, :]], axis=0)

    def conv_silu(blk, cs, pre):
        bs = slice(blk * d + cs.start, blk * d + cs.stop)
        x1 = shift_rows(pre, carry_scr[0, :, bs], 1)
        z = pre * convw_ref[1:2, bs] + x1 * convw_ref[0:1, bs]
        acc = (pre * convw_ref[3:4, bs] + x1 * convw_ref[2:3, bs]
               + shift_rows(z, carry_scr[1, :, bs], 2))
        carry_scr[0, :, bs] = pre[tm - CONV_PAD:tm, :]
        carry_scr[1, :, bs] = z[tm - CONV_PAD:tm, :]
        return _silu(acc)

    def l2_heads(cs, act, out_ref, scale):
        for hl in range(sw_w // hd):
            ah = act[:, hl * hd:(hl + 1) * hd]
            inv = lax.rsqrt(jnp.sum(ah * ah, axis=-1, keepdims=True) + NORM_EPS) * scale
            out_ref[0, :, cs.start + hl * hd:cs.start + (hl + 1) * hd] = (ah * inv).astype(BF16)

    def ep_q(cs, pre):
        l2_heads(cs, conv_silu(0, cs, pre), q_ref, hd ** -0.5)

    def ep_k(cs, pre):
        l2_heads(cs, conv_silu(1, cs, pre), k_ref, 1.0)

    def ep_vv(cs, pre):
        v_ref[0, :, cs] = conv_silu(2, cs, pre).astype(BF16)

    def ep_z(cs, pre):
        zs_ref[0, :, cs] = _silu(pre).astype(BF16)

    slabs = [slice(c0, c0 + sw_w) for c0 in range(0, d, sw_w)]

    def proj(w_ref, off):
        return lambda cs: _dot(hb, w_ref[:, off + cs.start:off + cs.stop])

    def proj_apart(cs):
        return (_dot(hb, wg_ref[:, cs]), _dot(ya_scr[...], wa_ref[:, cs]))

    stages = [(proj(wuv_ref, 0), ep_u), (proj(wuv_ref, d), ep_v), (proj(wg_ref, d), ep_gb),
              (proj(wqkv_ref, 0), ep_q), (proj_apart, ep_apart), (proj(wqkv_ref, d), ep_k),
              (proj(wqkv_ref, 2 * d), ep_vv), (proj(wz_ref, 0), ep_z)]
    items = [(mm, ep, cs) for mm, ep in stages for cs in slabs]
    pending = None
    for mm, ep, cs in items:
        pre = mm(cs)
        if pending is not None:
            pending[0](pending[1], pending[2])
        pending = (ep, cs, pre)
    pending[0](pending[1], pending[2])

    ab = _dot(hb, wab_ref[...])
    lane = lax.broadcasted_iota(jnp.int32, ab.shape, 1)
    is_beta = lane < HEADS
    g_col = -jnp.exp(alog16_ref[...]) * _softplus(ab + dtb16_ref[...])
    g_col = jnp.where(is_beta, 0.0, g_col)
    beta_col = jax.nn.sigmoid(ab)
    abt = _dot_nt(wabt_ref[...], hb)
    g_row = -jnp.exp(alogc_ref[...]) * _softplus(abt[HEADS:2 * HEADS, :] + dtbc_ref[...])
    gc = GDN_CHUNK
    ri = lax.broadcasted_iota(jnp.int32, (gc, gc), 0)
    ci = lax.broadcasted_iota(jnp.int32, (gc, gc), 1)
    lower = jnp.where(ci <= ri, 1.0, 0.0)
    upper = jnp.where(ri <= ci, 1.0, 0.0)
    is_beta_c = lax.broadcasted_iota(jnp.int32, (gc, 2 * HEADS), 1) < HEADS
    for n in range(tm // gc):
        rs = slice(n * gc, (n + 1) * gc)
        gam_col = jnp.dot(lower, g_col[rs, :], preferred_element_type=F32,
                          precision=lax.Precision.HIGHEST)
        gcol_ref[0, rs, :] = jnp.where(is_beta_c, beta_col[rs, :], gam_col)
        grow_ref[0, :, rs] = jnp.dot(g_row[:, rs], upper, preferred_element_type=F32,
                                     precision=lax.Precision.HIGHEST)


def _in_proj(x, mod, n1g, wuv, wqkv, wz, wab, wabt, wg, convw, sw, sbt, vg, alog16, dtb16,
             alogc, dtbc, wa, cast_f32, layer):
    B, T, D = x.shape
    tm = TM_IN
    nsteps = B * (T // tm)
    tok = lambda b, t: (b, t, 0)
    bf_out = jax.ShapeDtypeStruct((B, T, D), BF16)
    consts = [n1g, wuv, wqkv, wz, wab, wabt, wg, convw, sw, sbt, vg, alog16, dtb16, alogc,
              dtbc, wa]
    return pl.pallas_call(
        _in_proj_body,
        grid=(B, T // tm),
        in_specs=[pl.BlockSpec((1, tm, D), tok),
                  pl.BlockSpec((1, 6, D), lambda b, t: (b, 0, 0))]
        + [_const_spec(a.shape) for a in consts]
        + [_cast_spec(a.shape[1:], nsteps, T // tm, layer) for a in cast_f32],
        out_specs=[pl.BlockSpec((1, tm, D), tok)] * 6
        + [pl.BlockSpec((1, tm, 2 * HEADS), tok),
           pl.BlockSpec((1, HEADS, tm), lambda b, t: (b, 0, t))]
        + [_cast_spec(a.shape[1:], nsteps, T // tm) for a in cast_f32],
        out_shape=[bf_out] * 6
        + [jax.ShapeDtypeStruct((B, T, 2 * HEADS), F32),
           jax.ShapeDtypeStruct((B, HEADS, T), F32)]
        + [jax.ShapeDtypeStruct(a.shape[1:], BF16) for a in cast_f32],
        scratch_shapes=[pltpu.VMEM((2, CONV_PAD, 3 * D), F32),
                        pltpu.VMEM((tm, D), BF16),
                        pltpu.VMEM((tm, D), F32)],
        compiler_params=pltpu.CompilerParams(
            dimension_semantics=("arbitrary", "arbitrary"),
            vmem_limit_bytes=V7X_VMEM_LIMIT),
        name="in_proj",
    )(x, mod, *consts, *cast_f32)


def _gdn_body(q_ref, k_ref, v_ref, zs_ref, gcol_ref, grow_ref, og_ref, y_ref, s_scr):
    c = GDN_CHUNK
    nch = q_ref.shape[1] // c
    hd = HEAD_DIM
    t = pl.program_id(1)

    @pl.when(t == 0)
    def _():
        s_scr[...] = jnp.zeros(s_scr.shape, F32)

    row = lax.broadcasted_iota(jnp.int32, (c, c), 0)
    col = lax.broadcasted_iota(jnp.int32, (c, c), 1)
    incl = col <= row
    strict = col < row
    nblk = c // INV_BLOCK
    sh = INV_BLOCK.bit_length() - 1
    lane_c = lax.broadcasted_iota(jnp.int32, (INV_BLOCK, c), 1)
    row_c = lax.broadcasted_iota(jnp.int32, (INV_BLOCK, c), 0)
    lane_blk = lane_c >> sh
    lane_in = lane_c & (INV_BLOCK - 1)
    spread = jnp.where((row >> sh) == (col >> sh), 1.0, 0.0).astype(BF16)
    eye_c = jnp.where(lane_in == row_c, 1.0, 0.0)
    zero_b = jnp.zeros((), BF16)
    merge_masks = []
    s = INV_BLOCK
    while s < c:
        b = s.bit_length() - 1
        merge_masks.append((((row >> b) & 1) == 1) & ((col >> b) == (row >> b) - 1))
        s *= 2

    hs = range(HEADS)
    sl = [slice(h * hd, (h + 1) * hd) for h in hs]
    rows = [slice(n * c, (n + 1) * c) for n in range(nch)]
    nb = q_ref.shape[0]
    ps = range(nb * nch * HEADS)
    pb = [p // (nch * HEADS) for p in ps]
    pn = [(p // HEADS) % nch for p in ps]
    ph = [p % HEADS for p in ps]
    gcs = [[gcol_ref[b, rows[n], :] for n in range(nch)] for b in range(nb)]
    grs = [[grow_ref[b, :, rows[n]] for n in range(nch)] for b in range(nb)]
    gcp = [gcs[pb[p]][pn[p]] for p in ps]
    qh = [q_ref[pb[p], rows[pn[p]], sl[ph[p]]] for p in ps]
    kh = [k_ref[pb[p], rows[pn[p]], sl[ph[p]]] for p in ps]
    beta = [gcp[p][:, ph[p]:ph[p] + 1] for p in ps]
    gam = [gcp[p][:, HEADS + ph[p]:HEADS + ph[p] + 1] for p in ps]
    g_last = [gcp[p][c - 1:c, HEADS + ph[p]:HEADS + ph[p] + 1] for p in ps]
    gam_r = [grs[pb[p]][pn[p]][ph[p]:ph[p] + 1, :] for p in ps]
    kb = [kh[p].astype(F32) * beta[p] for p in ps]
    kq = [_dot_nt(jnp.concatenate([kb[p].astype(BF16), qh[p]], axis=0), kh[p]) for p in ps]
    kk = [kq[p][:c] for p in ps]
    qk = [kq[p][c:] for p in ps]
    dec = [jnp.where(incl, jnp.exp(jnp.minimum(gam[p] - gam_r[p], 0.0)), 0.0) for p in ps]
    a_mat = [jnp.where(strict, kk[p] * dec[p], 0.0).astype(BF16) for p in ps]
    qkd = [(qk[p] * dec[p]).astype(BF16) for p in ps]

    coef = []
    for p in ps:
        a_diag = jnp.zeros((INV_BLOCK, c), BF16)
        for blk in range(nblk):
            a_diag = a_diag + jnp.where(
                lane_blk == blk, a_mat[p][blk * INV_BLOCK:(blk + 1) * INV_BLOCK, :], zero_b)
        picked = jnp.concatenate(
            [jnp.where(lane_in == j, a_diag, zero_b) for j in range(INV_BLOCK)], axis=0)
        coef.append(_dot(picked, spread))
    x_inv = [eye_c for _ in ps]
    for j in range(INV_BLOCK - 1):
        x_inv = [x_inv[p] - coef[p][j * INV_BLOCK:(j + 1) * INV_BLOCK, :] * x_inv[p][j:j + 1, :]
                 for p in ps]
    t_mat = [jnp.concatenate([jnp.where(lane_blk == blk, x_inv[p], 0.0) for blk in range(nblk)],
                             axis=0) for p in ps]
    def odd_rows(m, bs):
        return jnp.concatenate([m[r0:r0 + bs, :] for r0 in range(bs, c, 2 * bs)], axis=0)

    def with_odd_rows(even, odd, bs):
        pieces = []
        for i, r0 in enumerate(range(0, c, 2 * bs)):
            pieces += [even[r0:r0 + bs, :], odd[i * bs:(i + 1) * bs, :]]
        return jnp.concatenate(pieces, axis=0)

    zeros_cc = jnp.zeros((c, c), BF16)
    bs = INV_BLOCK
    for mask in merge_masks:
        mask_odd = odd_rows(mask, bs)
        tb = [t_mat[p].astype(BF16) for p in ps]
        inner = [_dot(jnp.where(mask_odd, odd_rows(a_mat[p], bs), zero_b), tb[p])
                 for p in ps]
        inner = [with_odd_rows(zeros_cc, inner[p].astype(BF16), bs) for p in ps]
        t_odd = [odd_rows(t_mat[p], bs) - _dot(odd_rows(tb[p], bs), inner[p]) for p in ps]
        t_mat = [with_odd_rows(t_mat[p], t_odd[p], bs) for p in ps]
        bs *= 2

    e_gam = [jnp.exp(gam[p]) for p in ps]
    rhs = [jnp.concatenate([v_ref[pb[p], rows[pn[p]], sl[ph[p]]].astype(F32) * beta[p],
                            kb[p] * e_gam[p]], axis=1).astype(BF16) for p in ps]
    uw = [_dot(t_mat[p].astype(BF16), rhs[p]) for p in ps]
    wq = [jnp.concatenate([uw[p][:, hd:], qh[p].astype(F32) * e_gam[p]], axis=0).astype(BF16)
          for p in ps]
    k_dec = [(kh[p].astype(F32) * jnp.exp(g_last[p] - gam[p])).astype(BF16) for p in ps]

    bh = [(b, h) for b in range(nb) for h in hs]
    ix = range(len(bh))
    s_cur = [s_scr[b, h] for b, h in bh]
    for n in range(nch):
        pp = [(b * nch + n) * HEADS + h for b, h in bh]
        sb = [s_cur[i].astype(BF16) for i in ix]
        v_new = [(uw[pp[i]][:, :hd] - _dot(wq[pp[i]][:c], sb[i])).astype(BF16) for i in ix]
        o = [_dot(jnp.concatenate([wq[pp[i]][c:], qkd[pp[i]]], axis=1),
                  jnp.concatenate([sb[i], v_new[i]], axis=0)) for i in ix]
        s_cur = [s_cur[i] * jnp.exp(g_last[pp[i]]) + _dot_tn(k_dec[pp[i]], v_new[i]) for i in ix]
        for i, (b, h) in enumerate(bh):
            on = o[i] * lax.rsqrt(jnp.mean(o[i] * o[i], axis=-1, keepdims=True) + NORM_EPS)
            y_ref[b, rows[n], sl[h]] = (
                on * og_ref[...] * zs_ref[b, rows[n], sl[h]].astype(F32)).astype(BF16)
    for i, (b, h) in enumerate(bh):
        s_scr[b, h] = s_cur[i]


def _gdn(q, k, v, zs, gcol, grow, og):
    B, T, D = q.shape
    c = GDN_CHUNK * GDN_STEP_CHUNKS
    nb = GDN_STEP_BATCH
    tok = lambda b, t: (b, t, 0)
    return pl.pallas_call(
        _gdn_body,
        grid=(B // nb, T // c),
        in_specs=[pl.BlockSpec((nb, c, D), tok)] * 4
        + [pl.BlockSpec((nb, c, 2 * HEADS), tok),
           pl.BlockSpec((nb, HEADS, c), lambda b, t: (b, 0, t)),
           _const_spec(og.shape)],
        out_specs=pl.BlockSpec((nb, c, D), tok),
        out_shape=jax.ShapeDtypeStruct((B, T, D), BF16),
        scratch_shapes=[pltpu.VMEM((nb, HEADS, HEAD_DIM, HEAD_DIM), F32)],
        compiler_params=pltpu.CompilerParams(
            dimension_semantics=("arbitrary", "arbitrary")),
        name="gdn",
    )(q, k, v, zs, gcol, grow, og)


def _out_ffn_body(x_ref, apart_ref, gb_ref, yb_ref, mod_ref, wb_ref, wo_ref, n2g_ref,
                  wfi_ref, wfo_ref, fg_ref, o_ref, *, final_norm, ffn_split):
    f = wfo_ref.shape[0]
    x = x_ref[0]
    merged = apart_ref[0].astype(F32) + gb_ref[0].astype(F32) * _dot(yb_ref[0], wb_ref[...])
    x = x + mod_ref[0, 2:3, :] * _dot(merged.astype(BF16), wo_ref[...])
    ms = jnp.mean(x * x, axis=-1, keepdims=True)
    h = x * lax.rsqrt(ms + NORM_EPS) * n2g_ref[...]
    h = h * (1.0 + mod_ref[0, 4:5, :]) + mod_ref[0, 3:4, :]
    hb = h.astype(BF16)
    fc = f // ffn_split

    def gate_up(j):
        return (_dot(hb, wfi_ref[:, j * fc:(j + 1) * fc]),
                _dot(hb, wfi_ref[:, f + j * fc:f + (j + 1) * fc]))

    nxt = gate_up(0)
    acc = None
    for j in range(ffn_split):
        gate, up = nxt
        if j + 1 < ffn_split:
            nxt = gate_up(j + 1)
        part = _dot((_silu(gate) * up).astype(BF16), wfo_ref[j * fc:(j + 1) * fc, :])
        acc = part if acc is None else acc + part
    x = x + mod_ref[0, 5:6, :] * acc
    if final_norm:
        ms = jnp.mean(x * x, axis=-1, keepdims=True)
        x = x * lax.rsqrt(ms + NORM_EPS) * fg_ref[...]
    o_ref[0] = x


def _out_ffn(x, apart, gb, yb, mod, wb, wo, n2g, wfi, wfo, fg, final_norm):
    B, T, D = x.shape
    tm = TM_OUT
    tok = lambda b, t: (b, t, 0)
    consts = [wb, wo, n2g, wfi, wfo, fg]
    return pl.pallas_call(
        functools.partial(_out_ffn_body, final_norm=final_norm, ffn_split=FFN_SPLIT),
        grid=(B, T // tm),
        in_specs=[pl.BlockSpec((1, tm, D), tok)] * 4
        + [pl.BlockSpec((1, 6, D), lambda b, t: (b, 0, 0))]
        + [_const_spec(a.shape) for a in consts],
        out_specs=pl.BlockSpec((1, tm, D), tok),
        out_shape=jax.ShapeDtypeStruct((B, T, D), F32),
        compiler_params=pltpu.CompilerParams(
            dimension_semantics=("arbitrary", "arbitrary"),
            vmem_limit_bytes=V7X_VMEM_LIMIT),
        name="out_ffn",
    )(x, apart, gb, yb, mod, *consts)


def kernel(x, c, ada_w, ada_b, norm1_g, w_in, conv_w, spatial_w, spatial_b, v_norm_g, a_log,
           dt_bias, o_norm_g, w_branch_a, w_branch_b, w_out, norm2_g, w_ffn_in, w_ffn_out,
           final_g):
    B, T, D = x.shape
    L = ada_w.shape[0]
    assert D == A_GROUPS * A_CHUNK == HEADS * HEAD_DIM
    assert T % TM_IN == 0 and T % TM_OUT == 0 and TM_IN % GDN_CHUNK == 0
    assert B % GDN_STEP_BATCH == 0 and T % (GDN_CHUNK * GDN_STEP_CHUNKS) == 0
    assert conv_w.shape[1] == CONV_K

    mod_all = _ada_mod(c, ada_w, ada_b).reshape(L, B, 6, D)
    o_uv, o_qkv, o_z, o_ab, o_g = 0, 2 * D, 5 * D, 6 * D, 6 * D + 2 * HEADS
    zeros8 = jnp.zeros((HEADS,), F32)
    for i in range(L):
        wi = w_in[i]
        wab = wi[:, o_ab:o_g].astype(BF16)
        apart, gb, q, k, v, zs, gcol, grow, wb, wo, wfi, wfo = _in_proj(
            x, mod_all[i], norm1_g[i].reshape(1, D),
            wi[:, o_uv:o_qkv].astype(BF16), wi[:, o_qkv:o_z].astype(BF16),
            wi[:, o_z:o_ab].astype(BF16), wab, wab.T, wi[:, o_g:].astype(BF16),
            conv_w[i], spatial_w[i], spatial_b[i].T, v_norm_g[i].reshape(1, D),
            jnp.concatenate([zeros8, a_log[i]]).reshape(1, 2 * HEADS),
            jnp.concatenate([zeros8, dt_bias[i]]).reshape(1, 2 * HEADS),
            a_log[i].reshape(HEADS, 1), dt_bias[i].reshape(HEADS, 1),
            w_branch_a[i].astype(BF16), [w_branch_b, w_out, w_ffn_in, w_ffn_out], i)
        yb = _gdn(q, k, v, zs, gcol, grow, o_norm_g[i].reshape(1, HEAD_DIM))
        x = _out_ffn(x, apart, gb, yb, mod_all[i], wb, wo, norm2_g[i].reshape(1, D), wfi, wfo,
                     final_g.reshape(1, D), final_norm=(i == L - 1))
    return x
```

```python
import functools
import math

import jax
import jax.numpy as jnp
from jax import lax
from jax.experimental import pallas as pl
from jax.experimental.pallas import tpu as pltpu

NORM_EPS = 1e-6
A_GROUPS = 8
A_CHUNK = 128
HEADS = 8
HEAD_DIM = 128
CONV_K = 4
GDN_CHUNK = 128
INV_BLOCK = 16
BF16_SUBLANES = 16
CONV_PAD = 8

V7X_VMEM_LIMIT = 56 * 1024 * 1024

TM_IN = 512
TM_OUT = 512
GDN_STEP_CHUNKS = 2
GDN_STEP_BATCH = 2
IN_SLAB = 256
ADA_BLOCK = 2048
FFN_SPLIT = 11

BF16 = jnp.bfloat16
F32 = jnp.float32
LOG2E = 1.4426950408889634


def _dot(a, b):
    return jnp.dot(a, b, preferred_element_type=F32)


def _dot_nt(a, b):
    return lax.dot_general(a, b, (((1,), (1,)), ((), ())), preferred_element_type=F32)


def _dot_tn(a, b):
    return lax.dot_general(a, b, (((0,), (0,)), ((), ())), preferred_element_type=F32)


def _softplus(x):
    return jnp.maximum(x, 0.0) + jnp.log1p(jnp.exp(-jnp.abs(x)))


def _gelu_tanh(x):
    a = -2.0 * 0.7978845608028654 * LOG2E
    return x / (1.0 + jnp.exp2(x * (a + (a * 0.044715) * (x * x))))


def _silu(x):
    return x * jax.nn.sigmoid(x)


def _const_spec(shape):
    nd = len(shape)
    return pl.BlockSpec(shape, lambda *_: (0,) * nd, pipeline_mode=pl.Buffered(1))


def _cast_blocks(rows, nsteps):
    nblk = math.gcd(rows // BF16_SUBLANES, nsteps)
    return rows // nblk, nsteps // nblk


def _cast_spec(shape, nsteps, nt, layer=None):
    rb, rep = _cast_blocks(shape[0], nsteps)
    if layer is None:
        return pl.BlockSpec((rb, shape[1]), lambda b, t: ((b * nt + t) // rep, 0))
    return pl.BlockSpec((None, rb, shape[1]), lambda b, t: (layer, (b * nt + t) // rep, 0))


def _ada_body(c_ref, w_ref, b_ref, o_ref):
    cond = _silu(c_ref[...]).astype(BF16)
    o_ref[0] = _dot(cond, w_ref[0].astype(BF16)) + b_ref[0]


def _ada_mod(c, ada_w, ada_b):
    L, D, D6 = ada_w.shape
    B = c.shape[0]
    nblk = D6 // ADA_BLOCK
    return pl.pallas_call(
        _ada_body,
        grid=(L, nblk),
        in_specs=[
            pl.BlockSpec((B, D), lambda l, j: (0, 0)),
            pl.BlockSpec((1, D, ADA_BLOCK), lambda l, j: (l, 0, j)),
            pl.BlockSpec((1, 1, ADA_BLOCK), lambda l, j: (l, 0, j)),
        ],
        out_specs=pl.BlockSpec((1, B, ADA_BLOCK), lambda l, j: (l, 0, j)),
        out_shape=jax.ShapeDtypeStruct((L, B, D6), F32),
        name="ada_mod",
    )(c, ada_w, ada_b.reshape(L, 1, D6))


def _in_proj_body(x_ref, mod_ref, n1g_ref, wuv_ref, wqkv_ref, wz_ref, wab_ref, wabt_ref,
                  wg_ref, convw_ref, sw_ref, sbt_ref, vg_ref, alog16_ref, dtb16_ref,
                  alogc_ref, dtbc_ref, wa_ref, wb_f, wo_f, wfi_f, wfo_f,
                  apart_ref, gb_ref, q_ref, k_ref, v_ref, zs_ref, gcol_ref, grow_ref,
                  wb_o, wo_o, wfi_o, wfo_o,
                  carry_scr, ya_scr, u_scr):
    tm, d = x_ref.shape[1], x_ref.shape[2]
    t = pl.program_id(1)

    @pl.when(t == 0)
    def _():
        carry_scr[...] = jnp.zeros(carry_scr.shape, F32)

    for src, dst in ((wb_f, wb_o), (wo_f, wo_o), (wfi_f, wfi_o), (wfo_f, wfo_o)):
        dst[...] = src[...].astype(BF16)

    x = x_ref[0]
    ms = jnp.mean(x * x, axis=-1, keepdims=True)
    h = x * lax.rsqrt(ms + NORM_EPS) * n1g_ref[...]
    h = h * (1.0 + mod_ref[0, 1:2, :]) + mod_ref[0, 0:1, :]
    hb = h.astype(BF16)

    hd = HEAD_DIM
    gd = d // A_GROUPS
    sw_w = IN_SLAB
    row = lax.broadcasted_iota(jnp.int32, (A_CHUNK, A_CHUNK), 0)
    col = lax.broadcasted_iota(jnp.int32, (A_CHUNK, A_CHUNK), 1)
    causal = col <= row

    def ep_u(cs, pre):
        u_scr[:, cs] = _gelu_tanh(pre)

    def ep_v(cs, pre):
        v = _gelu_tanh(pre)
        for gl in range(sw_w // gd):
            g = cs.start // gd + gl
            gs = slice(g * gd, (g + 1) * gd)
            vg = v[:, gl * gd:(gl + 1) * gd]
            vn = vg * lax.rsqrt(jnp.mean(vg * vg, axis=-1, keepdims=True) + NORM_EPS)
            vn = (vn * vg_ref[:, gs]).astype(BF16)
            wm = jnp.where(causal, sw_ref[g], 0.0).astype(BF16)
            bias = sbt_ref[:, g:g + 1]
            for n in range(tm // A_CHUNK):
                rs = slice(n * A_CHUNK, (n + 1) * A_CHUNK)
                s = _dot(wm, vn[rs, :]) + bias
                ya_scr[rs, gs] = (u_scr[rs, gs] * s).astype(BF16)

    def ep_gb(cs, pre):
        gb_ref[0, :, cs] = jax.nn.sigmoid(pre).astype(BF16)

    def ep_apart(cs, pre):
        d_ga, d_a = pre
        apart_ref[0, :, cs] = (jax.nn.sigmoid(d_ga) * d_a).astype(BF16)

    row8 = lax.broadcasted_iota(jnp.int32, (CONV_PAD, sw_w), 0)

    def shift_rows(a, prev_tail, k):
        r = pltpu.roll(a, k, axis=0)
        head = jnp.where(row8 < k, pltpu.roll(prev_tail, k, axis=0), r[0:CONV_PAD, :])
        return jnp.concatenate([head, r[CONV_PAD:, :]], axis=0)

    def conv_silu(blk, cs, pre):
        bs = slice(blk * d + cs.start, blk * d + cs.stop)
        x1 = shift_rows(pre, carry_scr[0, :, bs], 1)
        z = pre * convw_ref[1:2, bs] + x1 * convw_ref[0:1, bs]
        acc = (pre * convw_ref[3:4, bs] + x1 * convw_ref[2:3, bs]
               + shift_rows(z, carry_scr[1, :, bs], 2))
        carry_scr[0, :, bs] = pre[tm - CONV_PAD:tm, :]
        carry_scr[1, :, bs] = z[tm - CONV_PAD:tm, :]
        return _silu(acc)

    def l2_heads(cs, act, out_ref, scale):
        for hl in range(sw_w // hd):
            ah = act[:, hl * hd:(hl + 1) * hd]
            inv = lax.rsqrt(jnp.sum(ah * ah, axis=-1, keepdims=True) + NORM_EPS) * scale
            out_ref[0, :, cs.start + hl * hd:cs.start + (hl + 1) * hd] = (ah * inv).astype(BF16)

    def ep_q(cs, pre):
        l2_heads(cs, conv_silu(0, cs, pre), q_ref, hd ** -0.5)

    def ep_k(cs, pre):
        l2_heads(cs, conv_silu(1, cs, pre), k_ref, 1.0)

    def ep_vv(cs, pre):
        v_ref[0, :, cs] = conv_silu(2, cs, pre).astype(BF16)

    def ep_z(cs, pre):
        zs_ref[0, :, cs] = _silu(pre).astype(BF16)

    slabs = [slice(c0, c0 + sw_w) for c0 in range(0, d, sw_w)]

    def proj(w_ref, off):
        return lambda cs: _dot(hb, w_ref[:, off + cs.start:off + cs.stop])

    def proj_apart(cs):
        return (_dot(hb, wg_ref[:, cs]), _dot(ya_scr[...], wa_ref[:, cs]))

    stages = [(proj(wqkv_ref, 0), ep_q), (proj(wuv_ref, 0), ep_u), (proj(wqkv_ref, d), ep_k),
              (proj(wuv_ref, d), ep_v), (proj(wqkv_ref, 2 * d), ep_vv), (proj(wg_ref, d), ep_gb),
              (proj(wz_ref, 0), ep_z), (proj_apart, ep_apart)]
    items = [(mm, ep, cs) for mm, ep in stages for cs in slabs]
    pending = None
    for mm, ep, cs in items:
        pre = mm(cs)
        if pending is not None:
            pending[0](pending[1], pending[2])
        pending = (ep, cs, pre)
    pending[0](pending[1], pending[2])

    ab = _dot(hb, wab_ref[...])
    lane = lax.broadcasted_iota(jnp.int32, ab.shape, 1)
    is_beta = lane < HEADS
    g_col = -jnp.exp(alog16_ref[...]) * _softplus(ab + dtb16_ref[...])
    g_col = jnp.where(is_beta, 0.0, g_col)
    beta_col = jax.nn.sigmoid(ab)
    abt = _dot_nt(wabt_ref[...], hb)
    g_row = -jnp.exp(alogc_ref[...]) * _softplus(abt[HEADS:2 * HEADS, :] + dtbc_ref[...])
    gc = GDN_CHUNK
    ri = lax.broadcasted_iota(jnp.int32, (gc, gc), 0)
    ci = lax.broadcasted_iota(jnp.int32, (gc, gc), 1)
    lower = jnp.where(ci <= ri, 1.0, 0.0)
    upper = jnp.where(ri <= ci, 1.0, 0.0)
    is_beta_c = lax.broadcasted_iota(jnp.int32, (gc, 2 * HEADS), 1) < HEADS
    for n in range(tm // gc):
        rs = slice(n * gc, (n + 1) * gc)
        gam_col = jnp.dot(lower, g_col[rs, :], preferred_element_type=F32,
                          precision=lax.Precision.HIGHEST)
        gcol_ref[0, rs, :] = jnp.where(is_beta_c, beta_col[rs, :], gam_col)
        grow_ref[0, :, rs] = jnp.dot(g_row[:, rs], upper, preferred_element_type=F32,
                                     precision=lax.Precision.HIGHEST)


def _in_proj(x, mod, n1g, wuv, wqkv, wz, wab, wabt, wg, convw, sw, sbt, vg, alog16, dtb16,
             alogc, dtbc, wa, cast_f32, layer):
    B, T, D = x.shape
    tm = TM_IN
    nsteps = B * (T // tm)
    tok = lambda b, t: (b, t, 0)
    bf_out = jax.ShapeDtypeStruct((B, T, D), BF16)
    consts = [n1g, wuv, wqkv, wz, wab, wabt, wg, convw, sw, sbt, vg, alog16, dtb16, alogc,
              dtbc, wa]
    return pl.pallas_call(
        _in_proj_body,
        grid=(B, T // tm),
        in_specs=[pl.BlockSpec((1, tm, D), tok),
                  pl.BlockSpec((1, 6, D), lambda b, t: (b, 0, 0))]
        + [_const_spec(a.shape) for a in consts]
        + [_cast_spec(a.shape[1:], nsteps, T // tm, layer) for a in cast_f32],
        out_specs=[pl.BlockSpec((1, tm, D), tok)] * 6
        + [pl.BlockSpec((1, tm, 2 * HEADS), tok),
           pl.BlockSpec((1, HEADS, tm), lambda b, t: (b, 0, t))]
        + [_cast_spec(a.shape[1:], nsteps, T // tm) for a in cast_f32],
        out_shape=[bf_out] * 6
        + [jax.ShapeDtypeStruct((B, T, 2 * HEADS), F32),
           jax.ShapeDtypeStruct((B, HEADS, T), F32)]
        + [jax.ShapeDtypeStruct(a.shape[1:], BF16) for a in cast_f32],
        scratch_shapes=[pltpu.VMEM((2, CONV_PAD, 3 * D), F32),
                        pltpu.VMEM((tm, D), BF16),
                        pltpu.VMEM((tm, D), F32)],
        compiler_params=pltpu.CompilerParams(
            dimension_semantics=("arbitrary", "arbitrary"),
            vmem_limit_bytes=V7X_VMEM_LIMIT),
        name="in_proj",
    )(x, mod, *consts, *cast_f32)


def _gdn_body(q_ref, k_ref, v_ref, zs_ref, gcol_ref, grow_ref, og_ref, y_ref, s_scr):
    c = GDN_CHUNK
    nch = q_ref.shape[1] // c
    hd = HEAD_DIM
    t = pl.program_id(1)

    @pl.when(t == 0)
    def _():
        s_scr[...] = jnp.zeros(s_scr.shape, F32)

    row = lax.broadcasted_iota(jnp.int32, (c, c), 0)
    col = lax.broadcasted_iota(jnp.int32, (c, c), 1)
    incl = col <= row
    strict = col < row
    nblk = c // INV_BLOCK
    sh = INV_BLOCK.bit_length() - 1
    lane_c = lax.broadcasted_iota(jnp.int32, (INV_BLOCK, c), 1)
    row_c = lax.broadcasted_iota(jnp.int32, (INV_BLOCK, c), 0)
    lane_blk = lane_c >> sh
    lane_in = lane_c & (INV_BLOCK - 1)
    spread = jnp.where((row >> sh) == (col >> sh), 1.0, 0.0).astype(BF16)
    eye_c = jnp.where(lane_in == row_c, 1.0, 0.0)
    zero_b = jnp.zeros((), BF16)
    merge_masks = []
    s = INV_BLOCK
    while s < c:
        b = s.bit_length() - 1
        merge_masks.append((((row >> b) & 1) == 1) & ((col >> b) == (row >> b) - 1))
        s *= 2

    hs = range(HEADS)
    sl = [slice(h * hd, (h + 1) * hd) for h in hs]
    rows = [slice(n * c, (n + 1) * c) for n in range(nch)]
    nb = q_ref.shape[0]
    ps = range(nb * nch * HEADS)
    pb = [p // (nch * HEADS) for p in ps]
    pn = [(p // HEADS) % nch for p in ps]
    ph = [p % HEADS for p in ps]
    gcs = [[gcol_ref[b, rows[n], :] for n in range(nch)] for b in range(nb)]
    grs = [[grow_ref[b, :, rows[n]] for n in range(nch)] for b in range(nb)]
    gcp = [gcs[pb[p]][pn[p]] for p in ps]
    qh = [q_ref[pb[p], rows[pn[p]], sl[ph[p]]] for p in ps]
    kh = [k_ref[pb[p], rows[pn[p]], sl[ph[p]]] for p in ps]
    beta = [gcp[p][:, ph[p]:ph[p] + 1] for p in ps]
    gam = [gcp[p][:, HEADS + ph[p]:HEADS + ph[p] + 1] for p in ps]
    g_last = [gcp[p][c - 1:c, HEADS + ph[p]:HEADS + ph[p] + 1] for p in ps]
    gam_r = [grs[pb[p]][pn[p]][ph[p]:ph[p] + 1, :] for p in ps]
    kb = [kh[p].astype(F32) * beta[p] for p in ps]
    kq = [_dot_nt(jnp.concatenate([kb[p].astype(BF16), qh[p]], axis=0), kh[p]) for p in ps]
    kk = [kq[p][:c] for p in ps]
    qk = [kq[p][c:] for p in ps]
    dec = [jnp.where(incl, jnp.exp(jnp.minimum(gam[p] - gam_r[p], 0.0)), 0.0) for p in ps]
    a_mat = [jnp.where(strict, kk[p] * dec[p], 0.0).astype(BF16) for p in ps]
    qkd = [(qk[p] * dec[p]).astype(BF16) for p in ps]

    coef = []
    for p in ps:
        a_diag = jnp.zeros((INV_BLOCK, c), BF16)
        for blk in range(nblk):
            a_diag = a_diag + jnp.where(
                lane_blk == blk, a_mat[p][blk * INV_BLOCK:(blk + 1) * INV_BLOCK, :], zero_b)
        picked = jnp.concatenate(
            [jnp.where(lane_in == j, a_diag, zero_b) for j in range(INV_BLOCK)], axis=0)
        coef.append(_dot(picked, spread))
    x_inv = [eye_c for _ in ps]
    for j in range(INV_BLOCK - 1):
        x_inv = [x_inv[p] - coef[p][j * INV_BLOCK:(j + 1) * INV_BLOCK, :] * x_inv[p][j:j + 1, :]
                 for p in ps]
    t_mat = [jnp.concatenate([jnp.where(lane_blk == blk, x_inv[p], 0.0) for blk in range(nblk)],
                             axis=0) for p in ps]
    def odd_rows(m, bs):
        return jnp.concatenate([m[r0:r0 + bs, :] for r0 in range(bs, c, 2 * bs)], axis=0)

    def with_odd_rows(even, odd, bs):
        pieces = []
        for i, r0 in enumerate(range(0, c, 2 * bs)):
            pieces += [even[r0:r0 + bs, :], odd[i * bs:(i + 1) * bs, :]]
        return jnp.concatenate(pieces, axis=0)

    zeros_cc = jnp.zeros((c, c), BF16)
    bs = INV_BLOCK
    for mask in merge_masks:
        mask_odd = odd_rows(mask, bs)
        tb = [t_mat[p].astype(BF16) for p in ps]
        inner = [_dot(jnp.where(mask_odd, odd_rows(a_mat[p], bs), zero_b), tb[p])
                 for p in ps]
        inner = [with_odd_rows(zeros_cc, inner[p].astype(BF16), bs) for p in ps]
        t_odd = [odd_rows(t_mat[p], bs) - _dot(odd_rows(tb[p], bs), inner[p]) for p in ps]
        t_mat = [with_odd_rows(t_mat[p], t_odd[p], bs) for p in ps]
        bs *= 2

    e_gam = [jnp.exp(gam[p]) for p in ps]
    rhs = [jnp.concatenate([v_ref[pb[p], rows[pn[p]], sl[ph[p]]].astype(F32) * beta[p],
                            kb[p] * e_gam[p]], axis=1).astype(BF16) for p in ps]
    uw = [_dot(t_mat[p].astype(BF16), rhs[p]) for p in ps]
    wq = [jnp.concatenate([uw[p][:, hd:], qh[p].astype(F32) * e_gam[p]], axis=0).astype(BF16)
          for p in ps]
    k_dec = [(kh[p].astype(F32) * jnp.exp(g_last[p] - gam[p])).astype(BF16) for p in ps]

    bh = [(b, h) for b in range(nb) for h in hs]
    ix = range(len(bh))
    s_cur = [s_scr[b, h] for b, h in bh]
    for n in range(nch):
        pp = [(b * nch + n) * HEADS + h for b, h in bh]
        sb = [s_cur[i].astype(BF16) for i in ix]
        v_new = [(uw[pp[i]][:, :hd] - _dot(wq[pp[i]][:c], sb[i])).astype(BF16) for i in ix]
        o = [_dot(jnp.concatenate([wq[pp[i]][c:], qkd[pp[i]]], axis=1),
                  jnp.concatenate([sb[i], v_new[i]], axis=0)) for i in ix]
        s_cur = [s_cur[i] * jnp.exp(g_last[pp[i]]) + _dot_tn(k_dec[pp[i]], v_new[i]) for i in ix]
        for i, (b, h) in enumerate(bh):
            on = o[i] * lax.rsqrt(jnp.mean(o[i] * o[i], axis=-1, keepdims=True) + NORM_EPS)
            y_ref[b, rows[n], sl[h]] = (
                on * og_ref[...] * zs_ref[b, rows[n], sl[h]].astype(F32)).astype(BF16)
    for i, (b, h) in enumerate(bh):
        s_scr[b, h] = s_cur[i]


def _gdn(q, k, v, zs, gcol, grow, og):
    B, T, D = q.shape
    c = GDN_CHUNK * GDN_STEP_CHUNKS
    nb = GDN_STEP_BATCH
    tok = lambda b, t: (b, t, 0)
    return pl.pallas_call(
        _gdn_body,
        grid=(B // nb, T // c),
        in_specs=[pl.BlockSpec((nb, c, D), tok)] * 4
        + [pl.BlockSpec((nb, c, 2 * HEADS), tok),
           pl.BlockSpec((nb, HEADS, c), lambda b, t: (b, 0, t)),
           _const_spec(og.shape)],
        out_specs=pl.BlockSpec((nb, c, D), tok),
        out_shape=jax.ShapeDtypeStruct((B, T, D), BF16),
        scratch_shapes=[pltpu.VMEM((nb, HEADS, HEAD_DIM, HEAD_DIM), F32)],
        compiler_params=pltpu.CompilerParams(
            dimension_semantics=("arbitrary", "arbitrary")),
        name="gdn",
    )(q, k, v, zs, gcol, grow, og)


def _out_ffn_body(x_ref, apart_ref, gb_ref, yb_ref, mod_ref, wb_ref, wo_ref, n2g_ref,
                  wfi_ref, wfo_ref, fg_ref, o_ref, *, final_norm, ffn_split):
    f = wfo_ref.shape[0]
    x = x_ref[0]
    merged = apart_ref[0].astype(F32) + gb_ref[0].astype(F32) * _dot(yb_ref[0], wb_ref[...])
    x = x + mod_ref[0, 2:3, :] * _dot(merged.astype(BF16), wo_ref[...])
    ms = jnp.mean(x * x, axis=-1, keepdims=True)
    h = x * lax.rsqrt(ms + NORM_EPS) * n2g_ref[...]
    h = h * (1.0 + mod_ref[0, 4:5, :]) + mod_ref[0, 3:4, :]
    hb = h.astype(BF16)
    fc = f // ffn_split

    def gate_up(j):
        return (_dot(hb, wfi_ref[:, j * fc:(j + 1) * fc]),
                _dot(hb, wfi_ref[:, f + j * fc:f + (j + 1) * fc]))

    nxt = gate_up(0)
    acc = None
    for j in range(ffn_split):
        gate, up = nxt
        if j + 1 < ffn_split:
            nxt = gate_up(j + 1)
        part = _dot((_silu(gate) * up).astype(BF16), wfo_ref[j * fc:(j + 1) * fc, :])
        acc = part if acc is None else acc + part
    x = x + mod_ref[0, 5:6, :] * acc
    if final_norm:
        ms = jnp.mean(x * x, axis=-1, keepdims=True)
        x = x * lax.rsqrt(ms + NORM_EPS) * fg_ref[...]
    o_ref[0] = x


def _out_ffn(x, apart, gb, yb, mod, wb, wo, n2g, wfi, wfo, fg, final_norm):
    B, T, D = x.shape
    tm = TM_OUT
    tok = lambda b, t: (b, t, 0)
    consts = [wb, wo, n2g, wfi, wfo, fg]
    return pl.pallas_call(
        functools.partial(_out_ffn_body, final_norm=final_norm, ffn_split=FFN_SPLIT),
        grid=(B, T // tm),
        in_specs=[pl.BlockSpec((1, tm, D), tok)] * 4
        + [pl.BlockSpec((1, 6, D), lambda b, t: (b, 0, 0))]
        + [_const_spec(a.shape) for a in consts],
        out_specs=pl.BlockSpec((1, tm, D), tok),
        out_shape=jax.ShapeDtypeStruct((B, T, D), F32),
        compiler_params=pltpu.CompilerParams(
            dimension_semantics=("arbitrary", "arbitrary"),
            vmem_limit_bytes=V7X_VMEM_LIMIT),
        name="out_ffn",
    )(x, apart, gb, yb, mod, *consts)


def kernel(x, c, ada_w, ada_b, norm1_g, w_in, conv_w, spatial_w, spatial_b, v_norm_g, a_log,
           dt_bias, o_norm_g, w_branch_a, w_branch_b, w_out, norm2_g, w_ffn_in, w_ffn_out,
           final_g):
    B, T, D = x.shape
    L = ada_w.shape[0]
    assert D == A_GROUPS * A_CHUNK == HEADS * HEAD_DIM
    assert T % TM_IN == 0 and T % TM_OUT == 0 and TM_IN % GDN_CHUNK == 0
    assert B % GDN_STEP_BATCH == 0 and T % (GDN_CHUNK * GDN_STEP_CHUNKS) == 0
    assert conv_w.shape[1] == CONV_K

    mod_all = _ada_mod(c, ada_w, ada_b).reshape(L, B, 6, D)
    o_uv, o_qkv, o_z, o_ab, o_g = 0, 2 * D, 5 * D, 6 * D, 6 * D + 2 * HEADS
    zeros8 = jnp.zeros((HEADS,), F32)
    for i in range(L):
        wi = w_in[i]
        wab = wi[:, o_ab:o_g].astype(BF16)
        apart, gb, q, k, v, zs, gcol, grow, wb, wo, wfi, wfo = _in_proj(
            x, mod_all[i], norm1_g[i].reshape(1, D),
            wi[:, o_uv:o_qkv].astype(BF16), wi[:, o_qkv:o_z].astype(BF16),
            wi[:, o_z:o_ab].astype(BF16), wab, wab.T, wi[:, o_g:].astype(BF16),
            conv_w[i], spatial_w[i], spatial_b[i].T, v_norm_g[i].reshape(1, D),
            jnp.concatenate([zeros8, a_log[i]]).reshape(1, 2 * HEADS),
            jnp.concatenate([zeros8, dt_bias[i]]).reshape(1, 2 * HEADS),
            a_log[i].reshape(HEADS, 1), dt_bias[i].reshape(HEADS, 1),
            w_branch_a[i].astype(BF16), [w_branch_b, w_out, w_ffn_in, w_ffn_out], i)
        yb = _gdn(q, k, v, zs, gcol, grow, o_norm_g[i].reshape(1, HEAD_DIM))
        x = _out_ffn(x, apart, gb, yb, mod_all[i], wb, wo, norm2_g[i].reshape(1, D), wfi, wfo,
                     final_g.reshape(1, D), final_norm=(i == L - 1))
    return x
```
